```python
import math
import jax, jax.numpy as jnp
from jax import lax
import numpy as np

D_MODEL = 1024
BATCH = 8
SEQ = 2048
DEPTH = 2
DEC_BATCH = 32
DEC_SEQ = 8
PAST_LEN = 16384
PAGE_SIZE = 128

D_A = D_MODEL // 2
D_B = D_MODEL // 2
CONV_A_WIDTH = 3
CONV_B_WIDTH = 31
N_HEADS = 8
HEAD_DIM = D_MODEL // N_HEADS
N_IDX_HEADS = 8
IDX_DIM = 64
TOPK_MAX = 256
QUERY_BLOCK = 32
NUM_BUCKETS = 32
MAX_DISTANCE = 128
D_FF = 4 * D_MODEL
EPS = 1e-6
N_CONV_LAYERS = (DEPTH + 1) // 2
N_ATTN_LAYERS = DEPTH // 2
CONV_IN_WIDTH = 3 * D_A + 2 * D_B
ATTN_IN_WIDTH = 3 * D_MODEL + N_IDX_HEADS * IDX_DIM + IDX_DIM + N_IDX_HEADS

kernel_name = "hybrid_conv_dsa_decoder_step"


def rmsnorm(x, g):
    xf = x.astype(jnp.float32)
    y = xf * lax.rsqrt(jnp.mean(xf * xf, axis=-1, keepdims=True) + EPS)
    return (y * g.astype(jnp.float32)).astype(x.dtype)


def layernorm(x, g, b):
    xf = x.astype(jnp.float32)
    mu = jnp.mean(xf, axis=-1, keepdims=True)
    var = jnp.mean(jnp.square(xf - mu), axis=-1, keepdims=True)
    y = (xf - mu) * lax.rsqrt(var + EPS) * g.astype(jnp.float32) + b.astype(jnp.float32)
    return y.astype(x.dtype)


def causal_dwconv(u, buf, w):
    xp = jnp.concatenate([buf.astype(u.dtype), u], axis=1)
    y = lax.conv_general_dilated(xp, w.astype(u.dtype)[:, None, :], (1,), 'VALID',
                                 dimension_numbers=('NWC', 'WIO', 'NWC'),
                                 feature_group_count=u.shape[-1])
    return y, xp[:, xp.shape[1] - (w.shape[0] - 1):]


def t5_bucket(n):
    n = jnp.maximum(n, 0)
    max_exact = NUM_BUCKETS // 2
    nf = jnp.maximum(n, 1).astype(jnp.float32)
    large = max_exact + (jnp.log(nf / max_exact) / math.log(MAX_DISTANCE / max_exact)
                         * (NUM_BUCKETS - max_exact)).astype(jnp.int32)
    large = jnp.minimum(large, NUM_BUCKETS - 1)
    return jnp.where(n < max_exact, n, large)


def sq_relu_mlp(h, w_up, w_down):
    return jnp.square(jax.nn.relu(h @ w_up)) @ w_down


def conv_mixer(h, buf_a, buf_b, w_in, conv_a_w, conv_b_w, conv_b_bias, ln_g, ln_b, w_out):
    proj = h @ w_in
    gate_b, gate_c, xa, glu_a, glu_g = jnp.split(
        proj, [D_A, 2 * D_A, 3 * D_A, 3 * D_A + D_B], axis=-1)
    ya, new_a = causal_dwconv(gate_c * xa, buf_a, conv_a_w)
    ya = gate_b * ya
    u = glu_a * jax.nn.sigmoid(glu_g)
    yb, new_b = causal_dwconv(u, buf_b, conv_b_w)
    yb = jax.nn.silu(layernorm(yb + conv_b_bias, ln_g, ln_b))
    return jnp.concatenate([ya, yb], axis=-1) @ w_out, new_a, new_b


def dsa_attention(q, qi, wi, kidx_all, fetch, q_pos, rel_bias):
    B, T, H, Dh = q.shape
    L = kidx_all.shape[1]
    topk = min(TOPK_MAX, L // 4)
    qb = QUERY_BLOCK if T % QUERY_BLOCK == 0 else T
    nb = T // qb
    k_pos = jnp.arange(L, dtype=jnp.int32)
    kidx_f = kidx_all.astype(jnp.float32)

    def blk(a):
        return jnp.moveaxis(a.reshape(B, nb, qb, *a.shape[2:]), 1, 0)

    def one(args):
        q_b, qi_b, wi_b, pos_b = args
        s = jnp.einsum('bqhd,bsd->bhqs', qi_b.astype(jnp.float32), kidx_f) * (IDX_DIM ** -0.5)
        score = jnp.einsum('bhqs,bqh->bqs', jax.nn.relu(s), wi_b.astype(jnp.float32))
        score = jnp.where(k_pos[None, None, :] <= pos_b[None, :, None], score, -jnp.inf)
        _, sel = lax.top_k(score, topk)
        valid = sel <= pos_b[None, :, None]
        k_sel, v_sel = fetch(sel)
        logits = jnp.einsum('bqhd,bqkhd->bqhk', q_b, k_sel).astype(jnp.float32) * (Dh ** -0.5)
        bias = rel_bias[t5_bucket(pos_b[None, :, None] - sel)]
        logits = logits + jnp.swapaxes(bias, -1, -2).astype(jnp.float32)
        logits = jnp.where(valid[:, :, None, :], logits, -jnp.inf)
        p = jax.nn.softmax(logits, axis=-1).astype(v_sel.dtype)
        return jnp.einsum('bqhk,bqkhd->bqhd', p, v_sel)

    out = lax.map(one, (blk(q), blk(qi), blk(wi), q_pos.reshape(nb, qb)))
    return jnp.moveaxis(out, 0, 1).reshape(B, T, H, Dh)


def attn_mixer(h, w_in, w_out, rel_bias, past):
    B, T, _ = h.shape
    proj = h @ w_in
    q, k, v, qi, ki, wi = jnp.split(
        proj, [D_MODEL, 2 * D_MODEL, 3 * D_MODEL, 3 * D_MODEL + N_IDX_HEADS * IDX_DIM,
               3 * D_MODEL + N_IDX_HEADS * IDX_DIM + IDX_DIM], axis=-1)
    q = q.reshape(B, T, N_HEADS, HEAD_DIM)
    k = k.reshape(B, T, N_HEADS, HEAD_DIM)
    v = v.reshape(B, T, N_HEADS, HEAD_DIM)
    qi = qi.reshape(B, T, N_IDX_HEADS, IDX_DIM)
    wi = wi * (N_IDX_HEADS ** -0.5)
    take = jax.vmap(lambda a, i: a[i])
    if past is None:
        q_pos = jnp.arange(T, dtype=jnp.int32)
        kidx_all = ki

        def fetch(sel):
            return take(k, sel), take(v, sel)
    else:
        ck, cv, cki, pt = past
        page = ck.shape[1]
        past_len = pt.shape[1] * page
        q_pos = past_len + jnp.arange(T, dtype=jnp.int32)
        kidx_all = jnp.concatenate([cki[pt].reshape(B, past_len, IDX_DIM).astype(ki.dtype), ki], axis=1)

        def fetch(sel):
            pidx = jnp.clip(sel, 0, past_len - 1)
            phys = take(pt, pidx // page)
            off = pidx % page
            nidx = jnp.clip(sel - past_len, 0, T - 1)
            is_past = (sel < past_len)[..., None, None]
            k_sel = jnp.where(is_past, ck[phys, off].astype(k.dtype), take(k, nidx))
            v_sel = jnp.where(is_past, cv[phys, off].astype(v.dtype), take(v, nidx))
            return k_sel, v_sel
    o = dsa_attention(q, qi, wi, kidx_all, fetch, q_pos, rel_bias)
    return o.reshape(B, T, D_MODEL) @ w_out, k, v, ki


def setup_inputs(seed: int = 0) -> dict:
    key = jax.random.key(seed)
    ks = jax.random.split(key, 32)
    nc, na = N_CONV_LAYERS, N_ATTN_LAYERS
    n_pages = PAST_LEN // PAGE_SIZE
    n_used = DEC_BATCH * n_pages
    n_pool = n_used + (n_used + 3) // 4
    f32 = jnp.float32

    def nrm(k, shape, s):
        return jax.random.normal(k, shape, f32) * s

    def gain(k, shape):
        return 1.0 + 0.02 * jax.random.normal(k, shape, f32)

    page_table = jax.random.permutation(ks[7], n_pool)[:n_used].reshape(DEC_BATCH, n_pages).astype(jnp.int32)
    return {
        "x_prompt": nrm(ks[0], (BATCH, SEQ, D_MODEL), 1.0),
        "x_sample": nrm(ks[1], (DEC_BATCH, DEC_SEQ, D_MODEL), 1.0),
        "state_conv_a": nrm(ks[2], (nc, DEC_BATCH, CONV_A_WIDTH - 1, D_A), 1.0),
        "state_conv_b": nrm(ks[3], (nc, DEC_BATCH, CONV_B_WIDTH - 1, D_B), 1.0),
        "cache_k": nrm(ks[4], (na, n_pool, PAGE_SIZE, N_HEADS, HEAD_DIM), 1.0),
        "cache_v": nrm(ks[5], (na, n_pool, PAGE_SIZE, N_HEADS, HEAD_DIM), 1.0),
        "cache_kidx": nrm(ks[6], (na, n_pool, PAGE_SIZE, IDX_DIM), 1.0),
        "page_table": page_table,
        "norm_conv": gain(ks[8], (nc, D_MODEL)),
        "w_in_conv": nrm(ks[9], (nc, D_MODEL, CONV_IN_WIDTH), D_MODEL ** -0.5),
        "conv_a_w": nrm(ks[10], (nc, CONV_A_WIDTH, D_A), CONV_A_WIDTH ** -0.5),
        "conv_b_w": nrm(ks[11], (nc, CONV_B_WIDTH, D_B), CONV_B_WIDTH ** -0.5),
        "conv_b_bias": nrm(ks[12], (nc, D_B), 0.02),
        "ln_b_gain": gain(ks[13], (nc, D_B)),
        "ln_b_bias": nrm(ks[14], (nc, D_B), 0.02),
        "w_out_conv": nrm(ks[15], (nc, D_A + D_B, D_MODEL), (D_A + D_B) ** -0.5),
        "norm_attn": gain(ks[16], (na, D_MODEL)),
        "w_in_attn": nrm(ks[17], (na, D_MODEL, ATTN_IN_WIDTH), D_MODEL ** -0.5),
        "w_out_attn": nrm(ks[18], (na, D_MODEL, D_MODEL), D_MODEL ** -0.5),
        "rel_bias": nrm(ks[19], (NUM_BUCKETS, N_HEADS), 0.5),
        "norm_mlp": gain(ks[20], (DEPTH, D_MODEL)),
        "w_up": nrm(ks[21], (DEPTH, D_MODEL, D_FF), D_MODEL ** -0.5),
        "w_down": nrm(ks[22], (DEPTH, D_FF, D_MODEL), D_FF ** -0.5),
        "norm_final": gain(ks[23], (D_MODEL,)),
    }


def reference(x_prompt, x_sample, state_conv_a, state_conv_b, cache_k, cache_v, cache_kidx, page_table,
              norm_conv, w_in_conv, conv_a_w, conv_b_w, conv_b_bias, ln_b_gain, ln_b_bias, w_out_conv,
              norm_attn, w_in_attn, w_out_attn, rel_bias, norm_mlp, w_up, w_down, norm_final):
    def run(x, sample):
        B = x.shape[0]
        new_a, new_b, new_k, new_v, new_ki = [], [], [], [], []
        for layer in range(DEPTH):
            i = layer // 2
            if layer % 2 == 0:
                if sample:
                    buf_a, buf_b = state_conv_a[i], state_conv_b[i]
                else:
                    buf_a = jnp.zeros((B, CONV_A_WIDTH - 1, D_A), x.dtype)
                    buf_b = jnp.zeros((B, CONV_B_WIDTH - 1, D_B), x.dtype)
                out, sa, sb = conv_mixer(rmsnorm(x, norm_conv[i]), buf_a, buf_b, w_in_conv[i], conv_a_w[i],
                                         conv_b_w[i], conv_b_bias[i], ln_b_gain[i], ln_b_bias[i], w_out_conv[i])
                new_a.append(sa)
                new_b.append(sb)
            else:
                past = (cache_k[i], cache_v[i], cache_kidx[i], page_table) if sample else None
                out, k, v, ki = attn_mixer(rmsnorm(x, norm_attn[i]), w_in_attn[i], w_out_attn[i], rel_bias, past)
                new_k.append(k)
                new_v.append(v)
                new_ki.append(ki)
            x = x + out
            x = x + sq_relu_mlp(rmsnorm(x, norm_mlp[layer]), w_up[layer], w_down[layer])
        y = rmsnorm(x, norm_final)
        return (y, jnp.stack(new_a), jnp.stack(new_b), jnp.stack(new_k), jnp.stack(new_v), jnp.stack(new_ki))

    y_prompt, ca_p, cb_p, k_p, v_p, ki_p = run(x_prompt, False)
    y_sample, ca_s, cb_s, k_s, v_s, ki_s = run(x_sample, True)
    return (y_prompt, y_sample, ca_p, cb_p, k_p, v_p, ki_p, ca_s, cb_s, k_s, v_s, ki_s)
```

```python
import functools
import math

import jax
import jax.numpy as jnp
from jax import lax
from jax.experimental import pallas as pl
from jax.experimental.pallas import tpu as pltpu

F32 = jnp.float32
BF16 = jnp.bfloat16
I32 = jnp.int32

D_MODEL = 1024
D_A = 512
D_B = 512
CONV_A_WIDTH = 3
CONV_B_WIDTH = 31
N_HEADS = 8
HEAD_DIM = 128
N_IDX_HEADS = 8
IDX_DIM = 64
TOPK_MAX = 256
NUM_BUCKETS = 32
MAX_DISTANCE = 128
D_FF = 4096
EPS = 1e-6
PAGE = 128

INT_MIN = -(2 ** 31)
INT_MAX = 2 ** 31 - 1
NEG_BIG = -1e30

VMEM_LIMIT_BYTES = 56 * 1024 * 1024

A_HALO = 8
B_HALO = 32

TQ = 256
TK = 256
KV_PAGES = 4
KV_CHUNK = KV_PAGES * PAGE
IDX_PAGES = 16
IDX_CHUNK = IDX_PAGES * PAGE


def _params(sem):
    return pltpu.CompilerParams(dimension_semantics=sem, vmem_limit_bytes=VMEM_LIMIT_BYTES)


def _const_spec(shape):
    nd = len(shape)
    return pl.BlockSpec(shape, lambda *_: (0,) * nd, pipeline_mode=pl.Buffered(1))


def _rms_bf16(x, g):
    ms = jnp.mean(x * x, axis=-1, keepdims=True)
    return ((x * lax.rsqrt(ms + EPS)) * g).astype(BF16)


def _sigmoid(x):
    return 1.0 / (1.0 + jnp.exp(-x))


def _conv_in_kernel(x_ref, g_ref, w_ref, gb_ref, cx_ref, u_ref):
    xn = _rms_bf16(x_ref[...], g_ref[...])
    proj = jnp.dot(xn, w_ref[...], preferred_element_type=F32)
    gb_ref[...] = proj[:, :D_A]
    cx_ref[...] = proj[:, D_A:2 * D_A] * proj[:, 2 * D_A:3 * D_A]
    u_ref[...] = proj[:, 3 * D_A:3 * D_A + D_B] * _sigmoid(proj[:, 3 * D_A + D_B:])


def _conv_in(x2d, g, w_bf, tm):
    m = x2d.shape[0]
    n = w_bf.shape[1]
    out = jax.ShapeDtypeStruct((m, D_A), F32)
    return pl.pallas_call(
        _conv_in_kernel,
        grid=(m // tm,),
        in_specs=[pl.BlockSpec((tm, D_MODEL), lambda i: (i, 0)),
                  _const_spec((1, D_MODEL)),
                  _const_spec((D_MODEL, n))],
        out_specs=[pl.BlockSpec((tm, D_A), lambda i: (i, 0))] * 3,
        out_shape=[out, out, out],
        compiler_params=_params(("parallel",)),
        name="conv_in",
    )(x2d, g, w_bf)


def _conv_kernel(gb_ref, cx_ref, u_ref, sa_ref, sb_ref, wa_ref, wb_ref, bias_ref, lng_ref, lnb_ref,
                 y_ref, na_ref, nb_ref, xa_s, xb_s, *, tt, rc):
    t = pl.program_id(1)

    @pl.when(t == 0)
    def _():
        xa_s[0:A_HALO, :] = sa_ref[0]
        xb_s[0:B_HALO, :] = sb_ref[0]

    @pl.when(t > 0)
    def _():
        xa_s[0:A_HALO, :] = xa_s[tt:tt + A_HALO, :]
        xb_s[0:B_HALO, :] = xb_s[tt:tt + B_HALO, :]

    xa_s[A_HALO:A_HALO + tt, :] = cx_ref[0]
    xb_s[B_HALO:B_HALO + tt, :] = u_ref[0]

    bias = bias_ref[...]
    lng = lng_ref[...]
    lnb = lnb_ref[...]
    a0 = A_HALO - (CONV_A_WIDTH - 1)
    b0 = B_HALO - (CONV_B_WIDTH - 1)
    for c in range(tt // rc):
        r0 = c * rc
        ya = wa_ref[0:1, :] * xa_s[a0 + r0:a0 + r0 + rc, :]
        for j in range(1, CONV_A_WIDTH):
            ya = ya + wa_ref[j:j + 1, :] * xa_s[a0 + j + r0:a0 + j + r0 + rc, :]
        y_ref[0, r0:r0 + rc, 0:D_A] = gb_ref[0, r0:r0 + rc, :] * ya

        yb = wb_ref[0:1, :] * xb_s[b0 + r0:b0 + r0 + rc, :]
        for j in range(1, CONV_B_WIDTH):
            yb = yb + wb_ref[j:j + 1, :] * xb_s[b0 + j + r0:b0 + j + r0 + rc, :]
        yb = yb + bias
        mu = jnp.mean(yb, axis=-1, keepdims=True)
        var = jnp.mean(jnp.square(yb - mu), axis=-1, keepdims=True)
        z = (yb - mu) * lax.rsqrt(var + EPS) * lng + lnb
        y_ref[0, r0:r0 + rc, D_A:D_A + D_B] = z * _sigmoid(z)

    na_ref[0] = xa_s[tt + A_HALO - (CONV_A_WIDTH - 1):tt + A_HALO, :]
    nb_ref[0] = xb_s[tt + B_HALO - (CONV_B_WIDTH - 1):tt + B_HALO, :]


def _conv(gb, cx, u, sa, sb, wa, wb, bias, lng, lnb, tt, rc):
    b, t, _ = cx.shape
    row = lambda i, j: (i, j, 0)
    per_b = lambda i, j: (i, 0, 0)
    return pl.pallas_call(
        functools.partial(_conv_kernel, tt=tt, rc=rc),
        grid=(b, t // tt),
        in_specs=[pl.BlockSpec((1, tt, D_A), row),
                  pl.BlockSpec((1, tt, D_A), row),
                  pl.BlockSpec((1, tt, D_B), row),
                  pl.BlockSpec((1, A_HALO, D_A), per_b),
                  pl.BlockSpec((1, B_HALO, D_B), per_b),
                  pl.BlockSpec((CONV_A_WIDTH, D_A), lambda i, j: (0, 0)),
                  pl.BlockSpec((CONV_B_WIDTH, D_B), lambda i, j: (0, 0)),
                  pl.BlockSpec((1, D_B), lambda i, j: (0, 0)),
                  pl.BlockSpec((1, D_B), lambda i, j: (0, 0)),
                  pl.BlockSpec((1, D_B), lambda i, j: (0, 0))],
        out_specs=[pl.BlockSpec((1, tt, D_MODEL), row),
                   pl.BlockSpec((1, CONV_A_WIDTH - 1, D_A), per_b),
                   pl.BlockSpec((1, CONV_B_WIDTH - 1, D_B), per_b)],
        out_shape=[jax.ShapeDtypeStruct((b, t, D_MODEL), F32),
                   jax.ShapeDtypeStruct((b, CONV_A_WIDTH - 1, D_A), F32),
                   jax.ShapeDtypeStruct((b, CONV_B_WIDTH - 1, D_B), F32)],
        scratch_shapes=[pltpu.VMEM((A_HALO + tt, D_A), F32),
                        pltpu.VMEM((B_HALO + tt, D_B), F32)],
        compiler_params=_params(("parallel", "arbitrary")),
        name="conv_mix",
    )(gb, cx, u, sa, sb, wa, wb, bias, lng, lnb)


def _out_mlp_kernel(a_ref, res_ref, wo_ref, g_ref, wu_ref, wd_ref, gf_ref, o_ref, *, final_norm, ff_chunk):
    x1 = res_ref[...] + jnp.dot(a_ref[...].astype(BF16), wo_ref[...], preferred_element_type=F32)
    xn = _rms_bf16(x1, g_ref[...])
    acc = x1
    for c in range(D_FF // ff_chunk):
        h = jnp.dot(xn, wu_ref[:, c * ff_chunk:(c + 1) * ff_chunk], preferred_element_type=F32)
        h = jnp.square(jnp.maximum(h, 0.0)).astype(BF16)
        acc = acc + jnp.dot(h, wd_ref[c * ff_chunk:(c + 1) * ff_chunk, :], preferred_element_type=F32)
    if final_norm:
        ms = jnp.mean(acc * acc, axis=-1, keepdims=True)
        acc = (acc * lax.rsqrt(ms + EPS)) * gf_ref[...]
    o_ref[...] = acc


def _out_mlp(a2d, res2d, wo_bf, g, wu_bf, wd_bf, gf, tm, final_norm):
    m = res2d.shape[0]
    row = lambda i: (i, 0)
    return pl.pallas_call(
        functools.partial(_out_mlp_kernel, final_norm=final_norm, ff_chunk=1024),
        grid=(m // tm,),
        in_specs=[pl.BlockSpec((tm, D_MODEL), row),
                  pl.BlockSpec((tm, D_MODEL), row),
                  _const_spec((D_MODEL, D_MODEL)),
                  _const_spec((1, D_MODEL)),
                  _const_spec((D_MODEL, D_FF)),
                  _const_spec((D_FF, D_MODEL)),
                  _const_spec((1, D_MODEL))],
        out_specs=pl.BlockSpec((tm, D_MODEL), row),
        out_shape=jax.ShapeDtypeStruct((m, D_MODEL), F32),
        compiler_params=_params(("parallel",)),
        name="out_mlp",
    )(a2d, res2d, wo_bf, g, wu_bf, wd_bf, gf)


def _attn_in_kernel(x_ref, g_ref, wqkv_ref, wqi_ref, wkw_ref,
                    qb_ref, k_ref, v_ref, kb_ref, vb_ref, qib_ref, kw_ref):
    xn = _rms_bf16(x_ref[...], g_ref[...])
    qb_ref[...] = jnp.dot(xn, wqkv_ref[:, 0:D_MODEL], preferred_element_type=F32).astype(BF16)
    k = jnp.dot(xn, wqkv_ref[:, D_MODEL:2 * D_MODEL], preferred_element_type=F32)
    kb_ref[...] = k.astype(BF16)
    v = jnp.dot(xn, wqkv_ref[:, 2 * D_MODEL:3 * D_MODEL], preferred_element_type=F32)
    vb_ref[...] = v.astype(BF16)
    for h in range(N_HEADS):
        k_ref[:, h, :] = k[:, h * HEAD_DIM:(h + 1) * HEAD_DIM]
        v_ref[:, h, :] = v[:, h * HEAD_DIM:(h + 1) * HEAD_DIM]
    qib_ref[...] = jnp.dot(xn, wqi_ref[...], preferred_element_type=F32).astype(BF16)
    kw_ref[...] = jnp.dot(xn, wkw_ref[...], preferred_element_type=F32)


def _attn_in(x2d, g, wqkv_bf, wqi_bf, wkw_bf, tm):
    m = x2d.shape[0]
    row = lambda i: (i, 0)
    row3 = lambda i: (i, 0, 0)
    nqi = N_IDX_HEADS * IDX_DIM
    return pl.pallas_call(
        _attn_in_kernel,
        grid=(m // tm,),
        in_specs=[pl.BlockSpec((tm, D_MODEL), row),
                  _const_spec((1, D_MODEL)),
                  _const_spec((D_MODEL, 3 * D_MODEL)),
                  _const_spec((D_MODEL, nqi)),
                  _const_spec((D_MODEL, 128))],
        out_specs=[pl.BlockSpec((tm, D_MODEL), row),
                   pl.BlockSpec((tm, N_HEADS, HEAD_DIM), row3),
                   pl.BlockSpec((tm, N_HEADS, HEAD_DIM), row3),
                   pl.BlockSpec((tm, D_MODEL), row),
                   pl.BlockSpec((tm, D_MODEL), row),
                   pl.BlockSpec((tm, nqi), row),
                   pl.BlockSpec((tm, 128), row)],
        out_shape=[jax.ShapeDtypeStruct((m, D_MODEL), BF16),
                   jax.ShapeDtypeStruct((m, N_HEADS, HEAD_DIM), F32),
                   jax.ShapeDtypeStruct((m, N_HEADS, HEAD_DIM), F32),
                   jax.ShapeDtypeStruct((m, D_MODEL), BF16),
                   jax.ShapeDtypeStruct((m, D_MODEL), BF16),
                   jax.ShapeDtypeStruct((m, nqi), BF16),
                   jax.ShapeDtypeStruct((m, 128), F32)],
        compiler_params=_params(("parallel",)),
        name="attn_in",
    )(x2d, g, wqkv_bf, wqi_bf, wkw_bf)


def _t5_bucket(d):
    n = jnp.maximum(d, 0)
    max_exact = NUM_BUCKETS // 2
    nf = jnp.maximum(n, 1).astype(F32)
    large = max_exact + (jnp.log(nf / max_exact) / math.log(MAX_DISTANCE / max_exact)
                         * (NUM_BUCKETS - max_exact)).astype(I32)
    large = jnp.minimum(large, NUM_BUCKETS - 1)
    return jnp.where(n < max_exact, n, large)


def _bias_of_bucket(bucket, relb_ref, h):
    val = jnp.full(bucket.shape, relb_ref[NUM_BUCKETS - 1, h], F32)
    for bk in range(NUM_BUCKETS - 2, -1, -1):
        val = jnp.where(bucket == bk, relb_ref[bk, h], val)
    return val


def _prompt_bias_kernel(relb_ref, o_ref):
    h = pl.program_id(0)
    r = lax.broadcasted_iota(I32, (TQ, 2 * TK), 0)
    c = lax.broadcasted_iota(I32, (TQ, 2 * TK), 1)
    o_ref[0] = _bias_of_bucket(_t5_bucket(r + TK - c), relb_ref, h)


def _prompt_bias(rel_bias):
    return pl.pallas_call(
        _prompt_bias_kernel,
        grid=(N_HEADS,),
        in_specs=[pl.BlockSpec(memory_space=pltpu.SMEM)],
        out_specs=pl.BlockSpec((1, TQ, 2 * TK), lambda h: (h, 0, 0)),
        out_shape=jax.ShapeDtypeStruct((N_HEADS, TQ, 2 * TK), F32),
        compiler_params=_params(("parallel",)),
        name="prompt_bias",
    )(rel_bias)


def _sample_bias_kernel(relb_ref, o_ref, *, t, past_len):
    j = pl.program_id(0)
    q = lax.broadcasted_iota(I32, (t, KV_CHUNK), 0)
    c = lax.broadcasted_iota(I32, (t, KV_CHUNK), 1)
    bucket = _t5_bucket(past_len + q - (j * KV_CHUNK + c))
    for h in range(N_HEADS):
        o_ref[0, h * t:(h + 1) * t, :] = _bias_of_bucket(bucket, relb_ref, h)


def _sample_bias(rel_bias, t, past_len):
    n_chunks = past_len // KV_CHUNK + 1
    return pl.pallas_call(
        functools.partial(_sample_bias_kernel, t=t, past_len=past_len),
        grid=(n_chunks,),
        in_specs=[pl.BlockSpec(memory_space=pltpu.SMEM)],
        out_specs=pl.BlockSpec((1, N_HEADS * t, KV_CHUNK), lambda j: (j, 0, 0)),
        out_shape=jax.ShapeDtypeStruct((n_chunks, N_HEADS * t, KV_CHUNK), F32),
        compiler_params=_params(("parallel",)),
        name="sample_bias",
    )(rel_bias)


def _sortable(score):
    bits = pltpu.bitcast(score, I32)
    return jnp.where(bits < 0, bits ^ INT_MAX, bits)


def _bisect(count_fn, n_bits, init, rows):
    def body(it, cur):
        trial = cur ^ lax.shift_left(jnp.int32(1), jnp.asarray(n_bits - 1 - it, I32))
        return jnp.where(count_fn(trial), trial, cur)
    return lax.fori_loop(0, n_bits, body, jnp.full((rows, 1), init, I32))


def _prompt_attn_kernel(relb_ref, q_ref, qi_ref, kw_ref, k_ref, v_ref, db_ref, o_ref,
                        keys_s, acc_s, m_s, l_s, *, topk):
    i = pl.program_id(1)
    nt = i + 1
    scale = HEAD_DIM ** -0.5
    nt_dims = (((1,), (1,)), ((), ()))

    qi = qi_ref[0]
    w = kw_ref[0, pl.ds(pl.multiple_of(i * TQ, TQ), TQ), IDX_DIM:IDX_DIM + N_IDX_HEADS]
    w = w * (N_IDX_HEADS ** -0.5) * (IDX_DIM ** -0.5)
    qi_h = [qi[:, h * IDX_DIM:(h + 1) * IDX_DIM] for h in range(N_IDX_HEADS)]
    w_h = [jnp.broadcast_to(w[:, h:h + 1], (TQ, TK)) for h in range(N_IDX_HEADS)]
    qpos = i * TQ + lax.broadcasted_iota(I32, (TQ, TK), 0)
    col = lax.broadcasted_iota(I32, (TQ, TK), 1)

    def score_tile(j, _):
        ki_t = kw_ref[0, pl.ds(pl.multiple_of(j * TK, TK), TK), 0:IDX_DIM].astype(BF16)
        score = jnp.zeros((TQ, TK), F32)
        for h in range(N_IDX_HEADS):
            s = lax.dot_general(qi_h[h], ki_t, nt_dims, preferred_element_type=F32)
            score = score + jnp.maximum(s, 0.0) * w_h[h]
        key = jnp.where(j * TK + col <= qpos, _sortable(score), INT_MIN)
        keys_s[j] = key
        return 0

    lax.fori_loop(0, nt, score_tile, 0)

    def count(pred_fn):
        def body(j, c):
            m = pred_fn(keys_s[j], j).astype(I32)
            return c + m[:, 0:128] + m[:, 128:256]
        c = lax.fori_loop(0, nt, body, jnp.zeros((TQ, 128), I32))
        return jnp.sum(c, axis=1, keepdims=True)

    thr = _bisect(lambda x: count(lambda kt, j: kt >= x) >= topk, 32, INT_MIN, TQ)
    n_gt = count(lambda kt, j: kt > thr)
    n_eq = count(lambda kt, j: (kt == thr) & (kt != INT_MIN))
    need = topk - n_gt
    def tie_bound():
        return _bisect(lambda x: count(lambda kt, j: (kt == thr) & (j * TK + col < x)) < need,
                       31, 0, TQ)
    has_tie = jnp.max(n_eq - need) > 0
    xb = lax.cond(has_tie, tie_bound, lambda: jnp.full((TQ, 1), INT_MAX, I32))

    q = q_ref[0]
    q_h = [q[:, h * HEAD_DIM:(h + 1) * HEAD_DIM] for h in range(N_HEADS)]
    acc_s[...] = jnp.zeros_like(acc_s)
    m_s[...] = jnp.full_like(m_s, NEG_BIG)
    l_s[...] = jnp.zeros_like(l_s)

    def attend(j, bias_fn):
        r0 = pl.multiple_of(j * TK, TK)
        kt = keys_s[j]
        sel = ((kt > thr) | ((kt == thr) & (j * TK + col <= xb))) & (kt != INT_MIN)
        for h in range(N_HEADS):
            k_t = k_ref[0, pl.ds(r0, TK), h * HEAD_DIM:(h + 1) * HEAD_DIM]
            v_t = v_ref[0, pl.ds(r0, TK), h * HEAD_DIM:(h + 1) * HEAD_DIM]
            lg = lax.dot_general(q_h[h], k_t, nt_dims, preferred_element_type=F32) * scale
            lg = jnp.where(sel, lg + bias_fn(h), -jnp.inf)
            m_old = m_s[h]
            m_new = jnp.maximum(m_old, jnp.max(lg, axis=1, keepdims=True))
            alpha = jnp.exp(m_old - m_new)
            p = jnp.exp(lg - m_new)
            l_s[h] = alpha * l_s[h] + jnp.sum(p, axis=1, keepdims=True)
            m_s[h] = m_new
            pv = jnp.dot(p.astype(BF16), v_t, preferred_element_type=F32)
            acc_s[:, h * HEAD_DIM:(h + 1) * HEAD_DIM] = alpha * acc_s[:, h * HEAD_DIM:(h + 1) * HEAD_DIM] + pv

    def far_tile(j, _):
        attend(j, lambda h: relb_ref[NUM_BUCKETS - 1, h])
        return 0

    lax.fori_loop(0, i - 1, far_tile, 0)

    @pl.when(i >= 1)
    def _():
        attend(i - 1, lambda h: db_ref[h, :, 0:TK])

    attend(i, lambda h: db_ref[h, :, TK:2 * TK])

    for h in range(N_HEADS):
        o_ref[0, :, h * HEAD_DIM:(h + 1) * HEAD_DIM] = (
            acc_s[:, h * HEAD_DIM:(h + 1) * HEAD_DIM] / l_s[h]).astype(BF16)


def _prompt_attn(rel_bias, qb, qib, kw, kb, vb, dbias, topk):
    b, t, _ = qb.shape
    nqi = N_IDX_HEADS * IDX_DIM
    assert TK >= 113
    return pl.pallas_call(
        functools.partial(_prompt_attn_kernel, topk=topk),
        grid=(b, t // TQ),
        in_specs=[pl.BlockSpec(memory_space=pltpu.SMEM),
                  pl.BlockSpec((1, TQ, D_MODEL), lambda bi, i: (bi, i, 0)),
                  pl.BlockSpec((1, TQ, nqi), lambda bi, i: (bi, i, 0)),
                  pl.BlockSpec((1, t, 128), lambda bi, i: (bi, 0, 0)),
                  pl.BlockSpec((1, t, D_MODEL), lambda bi, i: (bi, 0, 0)),
                  pl.BlockSpec((1, t, D_MODEL), lambda bi, i: (bi, 0, 0)),
                  _const_spec((N_HEADS, TQ, 2 * TK))],
        out_specs=pl.BlockSpec((1, TQ, D_MODEL), lambda bi, i: (bi, i, 0)),
        out_shape=jax.ShapeDtypeStruct((b, t, D_MODEL), BF16),
        scratch_shapes=[pltpu.VMEM((t // TK, TQ, TK), I32),
                        pltpu.VMEM((TQ, D_MODEL), F32),
                        pltpu.VMEM((N_HEADS, TQ, 1), F32),
                        pltpu.VMEM((N_HEADS, TQ, 1), F32)],
        compiler_params=_params(("parallel", "arbitrary")),
        name="prompt_attn",
    )(rel_bias, qb, qib, kw, kb, vb, dbias)


def _sample_attn_kernel(pt_ref, q_ref, qi_ref, kw_ref, kn_ref, vn_ref, sb_ref, cki_hbm, ck_hbm, cv_hbm,
                        o_ref, ibuf, kbuf, vbuf, keys_s, isem, ksem, vsem,
                        *, t, past_len, topk, page_base):
    b = pl.program_id(0)
    n_idx_chunks = past_len // IDX_CHUNK
    n_kv_chunks = past_len // KV_CHUNK
    n_tiles = past_len // 128 + 1
    rows = N_HEADS * t
    scale = HEAD_DIM ** -0.5
    nt_dims = (((1,), (1,)), ((), ()))

    def idx_copy(c, slot, p):
        page = page_base + pt_ref[b, c * IDX_PAGES + p]
        return pltpu.make_async_copy(cki_hbm.at[page], ibuf.at[slot, p], isem.at[slot])

    def kv_copies(c, slot, p):
        page = page_base + pt_ref[b, c * KV_PAGES + p]
        return (pltpu.make_async_copy(ck_hbm.at[page], kbuf.at[slot, pl.ds(p * PAGE, PAGE)], ksem.at[slot]),
                pltpu.make_async_copy(cv_hbm.at[page], vbuf.at[slot, pl.ds(p * PAGE, PAGE)], vsem.at[slot]))

    def start_idx(c, slot):
        for p in range(IDX_PAGES):
            idx_copy(c, slot, p).start()

    def start_kv(c, slot):
        for p in range(KV_PAGES):
            ck, cv = kv_copies(c, slot, p)
            ck.start()
            cv.start()

    start_idx(0, 0)
    start_kv(0, 0)

    qi = qi_ref[0].astype(F32)
    qi_rows = jnp.concatenate([qi[:, h * IDX_DIM:(h + 1) * IDX_DIM] for h in range(N_IDX_HEADS)],
                              axis=0).astype(BF16)
    w = kw_ref[0][:, IDX_DIM:IDX_DIM + N_IDX_HEADS] * (N_IDX_HEADS ** -0.5) * (IDX_DIM ** -0.5)
    w_h = [w[:, h:h + 1] for h in range(N_IDX_HEADS)]

    def scores(ki_bf, transposed):
        if transposed:
            s = jnp.dot(qi_rows, ki_bf, preferred_element_type=F32)
        else:
            s = lax.dot_general(qi_rows, ki_bf, nt_dims, preferred_element_type=F32)
        score = jnp.maximum(s[0:t], 0.0) * w_h[0]
        for h in range(1, N_IDX_HEADS):
            score = score + jnp.maximum(s[h * t:(h + 1) * t], 0.0) * w_h[h]
        return score

    def idx_chunk(c, _):
        slot = c % 2

        @pl.when(c + 1 < n_idx_chunks)
        def _():
            start_idx(c + 1, 1 - slot)

        for p in range(IDX_PAGES):
            idx_copy(c, slot, p).wait()
        ki_t = jnp.concatenate([ibuf[slot, p] for p in range(IDX_PAGES)], axis=1)
        key = _sortable(scores(ki_t.astype(BF16), True))
        for k in range(IDX_CHUNK // 128):
            keys_s[c * (IDX_CHUNK // 128) + k] = key[:, k * 128:(k + 1) * 128]
        return 0

    lax.fori_loop(0, n_idx_chunks, idx_chunk, 0)

    ki_new = jnp.concatenate([kw_ref[0][:, 0:IDX_DIM], jnp.zeros((128 - t, IDX_DIM), F32)], axis=0)
    qrow = lax.broadcasted_iota(I32, (t, 128), 0)
    lane = lax.broadcasted_iota(I32, (t, 128), 1)
    keys_s[n_tiles - 1] = jnp.where(lane <= qrow, _sortable(scores(ki_new.astype(BF16), False)), INT_MIN)

    pos = lax.broadcasted_iota(I32, (n_tiles, t, 128), 0) * 128 + lax.broadcasted_iota(I32, (n_tiles, t, 128), 2)

    def count(pred):
        return jnp.sum(jnp.sum(pred.astype(I32), axis=0), axis=1, keepdims=True)

    thr = _bisect(lambda x: count(keys_s[...] >= x[None]) >= topk, 32, INT_MIN, t)
    kall = keys_s[...]
    n_gt = count(kall > thr[None])
    need = topk - n_gt
    xb = _bisect(lambda x: count((keys_s[...] == thr[None]) & (pos < x[None])) < need, 31, 0, t)
    amask_s = jnp.where(((kall > thr[None]) | ((kall == thr[None]) & (pos <= xb[None]))) & (kall != INT_MIN),
                        0.0, -jnp.inf).astype(F32)
    keys_s[...] = pltpu.bitcast(amask_s, I32)

    q = q_ref[0].astype(F32)
    lane_head = lax.broadcasted_iota(I32, (t, D_MODEL), 1) // HEAD_DIM
    q_bd = jnp.concatenate([jnp.where(lane_head == h, q, 0.0) for h in range(N_HEADS)],
                           axis=0).astype(BF16)

    def heads_to_lanes(ref):
        return jnp.concatenate([ref[:, h, :] for h in range(N_HEADS)], axis=1).astype(BF16)

    def attend(carry, k_bf, v_bf, bias, amask):
        m_old, l_old, acc = carry
        lg = lax.dot_general(q_bd, k_bf, nt_dims, preferred_element_type=F32) * scale
        lg = lg + bias + jnp.concatenate([amask] * N_HEADS, axis=0)
        m_new = jnp.maximum(m_old, jnp.max(lg, axis=1, keepdims=True))
        alpha = jnp.exp(m_old - m_new)
        p = jnp.exp(lg - m_new)
        l_new = alpha * l_old + jnp.sum(p, axis=1, keepdims=True)
        acc = alpha * acc + jnp.dot(p.astype(BF16), v_bf, preferred_element_type=F32)
        return m_new, l_new, acc

    def kv_chunk(c, carry):
        slot = c % 2

        @pl.when(c + 1 < n_kv_chunks)
        def _():
            start_kv(c + 1, 1 - slot)

        for p in range(KV_PAGES):
            ck, cv = kv_copies(c, slot, p)
            ck.wait()
            cv.wait()
        tiles = KV_CHUNK // 128
        amask = jnp.concatenate(
            [pltpu.bitcast(keys_s[c * tiles + k], F32) for k in range(tiles)], axis=1)
        return attend(carry, heads_to_lanes(kbuf.at[slot]), heads_to_lanes(vbuf.at[slot]), sb_ref[c], amask)

    carry = (jnp.full((rows, 1), NEG_BIG, F32), jnp.zeros((rows, 1), F32), jnp.zeros((rows, D_MODEL), F32))
    carry = lax.fori_loop(0, n_kv_chunks, kv_chunk, carry)

    pad = jnp.zeros((128 - t, D_MODEL), BF16)
    k_new = jnp.concatenate([heads_to_lanes(kn_ref.at[0]), pad], axis=0)
    v_new = jnp.concatenate([heads_to_lanes(vn_ref.at[0]), pad], axis=0)
    _, l_fin, acc = attend(carry, k_new, v_new, sb_ref[n_kv_chunks][:, 0:128],
                           pltpu.bitcast(keys_s[n_tiles - 1], F32))
    out = acc / l_fin
    for h in range(N_HEADS):
        o_ref[0, :, h * HEAD_DIM:(h + 1) * HEAD_DIM] = out[h * t:(h + 1) * t,
                                                           h * HEAD_DIM:(h + 1) * HEAD_DIM].astype(BF16)


def _sample_attn(page_table, qb, qib, kw, k_new, v_new, sbias, cki, ck, cv, topk, page_base):
    b, t, _ = qb.shape
    past_len = page_table.shape[1] * PAGE
    nqi = N_IDX_HEADS * IDX_DIM
    n_tiles = past_len // 128 + 1
    per_b = lambda bi, pt: (bi, 0, 0)
    per_b4 = lambda bi, pt: (bi, 0, 0, 0)
    grid_spec = pltpu.PrefetchScalarGridSpec(
        num_scalar_prefetch=1,
        grid=(b,),
        in_specs=[pl.BlockSpec((1, t, D_MODEL), per_b),
                  pl.BlockSpec((1, t, nqi), per_b),
                  pl.BlockSpec((1, t, 128), per_b),
                  pl.BlockSpec((1, t, N_HEADS, HEAD_DIM), per_b4),
                  pl.BlockSpec((1, t, N_HEADS, HEAD_DIM), per_b4),
                  pl.BlockSpec(sbias.shape, lambda bi, pt: (0, 0, 0), pipeline_mode=pl.Buffered(1)),
                  pl.BlockSpec(memory_space=pl.ANY),
                  pl.BlockSpec(memory_space=pl.ANY),
                  pl.BlockSpec(memory_space=pl.ANY)],
        out_specs=pl.BlockSpec((1, t, D_MODEL), per_b),
        scratch_shapes=[pltpu.VMEM((2, IDX_PAGES, IDX_DIM, PAGE), F32),
                        pltpu.VMEM((2, KV_CHUNK, N_HEADS, HEAD_DIM), F32),
                        pltpu.VMEM((2, KV_CHUNK, N_HEADS, HEAD_DIM), F32),
                        pltpu.VMEM((n_tiles, t, 128), I32),
                        pltpu.SemaphoreType.DMA((2,)),
                        pltpu.SemaphoreType.DMA((2,)),
                        pltpu.SemaphoreType.DMA((2,))])
    return pl.pallas_call(
        functools.partial(_sample_attn_kernel, t=t, past_len=past_len, topk=topk, page_base=page_base),
        grid_spec=grid_spec,
        out_shape=jax.ShapeDtypeStruct((b, t, D_MODEL), BF16),
        compiler_params=_params(("arbitrary",)),
        name="sample_attn",
    )(page_table, qb, qib, kw, k_new, v_new, sbias, cki, ck, cv)


def kernel(x_prompt, x_sample, state_conv_a, state_conv_b, cache_k, cache_v, cache_kidx, page_table,
           norm_conv, w_in_conv, conv_a_w, conv_b_w, conv_b_bias, ln_b_gain, ln_b_bias, w_out_conv,
           norm_attn, w_in_attn, w_out_attn, rel_bias, norm_mlp, w_up, w_down, norm_final):
    depth = norm_mlp.shape[0]
    n_pool = cache_k.shape[1]
    past_len = page_table.shape[1] * PAGE
    nqkv = 3 * D_MODEL
    nqi = N_IDX_HEADS * IDX_DIM
    row = lambda a: a.reshape(1, -1)

    cki_flat = jnp.swapaxes(cache_kidx, 2, 3).reshape(-1, IDX_DIM, PAGE)
    ck_flat = cache_k.reshape(-1, PAGE, N_HEADS, HEAD_DIM)
    cv_flat = cache_v.reshape(-1, PAGE, N_HEADS, HEAD_DIM)
    dbias = _prompt_bias(rel_bias)
    sbias = _sample_bias(rel_bias, x_sample.shape[1], past_len)

    def run(x, sample):
        b, t, _ = x.shape
        m = b * t
        tm = min(512, m)
        x2d = x.reshape(m, D_MODEL)
        new_a, new_b, new_k, new_v, new_ki = [], [], [], [], []
        for layer in range(depth):
            i = layer // 2
            last = layer == depth - 1
            if layer % 2 == 0:
                if sample:
                    buf_a, buf_b = state_conv_a[i], state_conv_b[i]
                else:
                    buf_a = jnp.zeros((b, CONV_A_WIDTH - 1, D_A), F32)
                    buf_b = jnp.zeros((b, CONV_B_WIDTH - 1, D_B), F32)
                sa = jnp.pad(buf_a, ((0, 0), (A_HALO - (CONV_A_WIDTH - 1), 0), (0, 0)))
                sb = jnp.pad(buf_b, ((0, 0), (B_HALO - (CONV_B_WIDTH - 1), 0), (0, 0)))
                gb, cx, u = _conv_in(x2d, row(norm_conv[i]), w_in_conv[i].astype(BF16), tm)
                tt = min(512, t)
                y, na, nb = _conv(gb.reshape(b, t, D_A), cx.reshape(b, t, D_A), u.reshape(b, t, D_B), sa, sb,
                                  conv_a_w[i], conv_b_w[i], row(conv_b_bias[i]), row(ln_b_gain[i]),
                                  row(ln_b_bias[i]), tt, min(64, tt))
                new_a.append(na)
                new_b.append(nb)
                mix = y.reshape(m, D_MODEL)
                w_out = w_out_conv[i]
            else:
                w_in = w_in_attn[i]
                wkw = jnp.pad(w_in[:, nqkv + nqi:], ((0, 0), (0, 128 - IDX_DIM - N_IDX_HEADS)))
                qb, k, v, kb, vb, qib, kw = _attn_in(
                    x2d, row(norm_attn[i]), w_in[:, :nqkv].astype(BF16),
                    w_in[:, nqkv:nqkv + nqi].astype(BF16), wkw.astype(BF16), tm)
                shp = lambda a: a.reshape(b, t, a.shape[-1])
                if sample:
                    topk = min(TOPK_MAX, (past_len + t) // 4)
                    o = _sample_attn(page_table, shp(qb), shp(qib), shp(kw),
                                     k.reshape(b, t, N_HEADS, HEAD_DIM), v.reshape(b, t, N_HEADS, HEAD_DIM),
                                     sbias, cki_flat, ck_flat, cv_flat, topk, i * n_pool)
                else:
                    topk = min(TOPK_MAX, t // 4)
                    o = _prompt_attn(rel_bias, shp(qb), shp(qib), shp(kw), shp(kb), shp(vb), dbias, topk)
                new_k.append(k.reshape(b, t, N_HEADS, HEAD_DIM))
                new_v.append(v.reshape(b, t, N_HEADS, HEAD_DIM))
                new_ki.append(kw[:, :IDX_DIM].reshape(b, t, IDX_DIM))
                mix = o.reshape(m, D_MODEL)
                w_out = w_out_attn[i]
            x2d = _out_mlp(mix, x2d, w_out.astype(BF16), row(norm_mlp[layer]), w_up[layer].astype(BF16),
                           w_down[layer].astype(BF16), row(norm_final), tm, last)
        return (x2d.reshape(b, t, D_MODEL), jnp.stack(new_a), jnp.stack(new_b),
                jnp.stack(new_k), jnp.stack(new_v), jnp.stack(new_ki))

    y_p, ca_p, cb_p, k_p, v_p, ki_p = run(x_prompt, False)
    y_s, ca_s, cb_s, k_s, v_s, ki_s = run(x_sample, True)
    return (y_p, y_s, ca_p, cb_p, k_p, v_p, ki_p, ca_s, cb_s, k_s, v_s, ki_s)
```

```python
import functools
import math

import jax
import jax.numpy as jnp
from jax import lax
from jax.experimental import pallas as pl
from jax.experimental.pallas import tpu as pltpu

F32 = jnp.float32
BF16 = jnp.bfloat16
I32 = jnp.int32

D_MODEL = 1024
D_A = 512
D_B = 512
CONV_A_WIDTH = 3
CONV_B_WIDTH = 31
N_HEADS = 8
HEAD_DIM = 128
N_IDX_HEADS = 8
IDX_DIM = 64
TOPK_MAX = 256
NUM_BUCKETS = 32
MAX_DISTANCE = 128
D_FF = 4096
EPS = 1e-6
PAGE = 128
LANES = 128

INT_MIN = -(2 ** 31)
INT_MAX = 2 ** 31 - 1
NEG_BIG = -1e30

VMEM_LIMIT_BYTES = 56 * 1024 * 1024

A_HALO = 8
B_HALO = 32

TQ = 256
TK = 128
NEAR_TILES = TQ // TK + 1
LOG2E = math.log2(math.e)
QK_SCALE = (HEAD_DIM ** -0.5) * LOG2E
KV_PAGES = 4
IDX_PAGES = 16
IDX_CHUNK = IDX_PAGES * PAGE
PAGE_ROWS = PAGE * N_HEADS


def _params(sem):
    return pltpu.CompilerParams(dimension_semantics=sem, vmem_limit_bytes=VMEM_LIMIT_BYTES)


def _const_spec(shape):
    nd = len(shape)
    return pl.BlockSpec(shape, lambda *_: (0,) * nd, pipeline_mode=pl.Buffered(1))


def _rms_bf16(x, g):
    ms = jnp.mean(x * x, axis=-1, keepdims=True)
    return ((x * lax.rsqrt(ms + EPS)) * g).astype(BF16)


def _sigmoid(x):
    return 1.0 / (1.0 + jnp.exp(-x))


def _conv_in_kernel(x_ref, g_ref, w_ref, gb_ref, cx_ref, u_ref):
    xn = _rms_bf16(x_ref[...], g_ref[...])
    proj = jnp.dot(xn, w_ref[...], preferred_element_type=F32)
    gb_ref[...] = proj[:, :D_A]
    cx_ref[...] = proj[:, D_A:2 * D_A] * proj[:, 2 * D_A:3 * D_A]
    u_ref[...] = proj[:, 3 * D_A:3 * D_A + D_B] * _sigmoid(proj[:, 3 * D_A + D_B:])


def _conv_in(x2d, g, w_bf, tm):
    m = x2d.shape[0]
    n = w_bf.shape[1]
    out = jax.ShapeDtypeStruct((m, D_A), F32)
    return pl.pallas_call(
        _conv_in_kernel,
        grid=(m // tm,),
        in_specs=[pl.BlockSpec((tm, D_MODEL), lambda i: (i, 0)),
                  _const_spec((1, D_MODEL)),
                  _const_spec((D_MODEL, n))],
        out_specs=[pl.BlockSpec((tm, D_A), lambda i: (i, 0))] * 3,
        out_shape=[out, out, out],
        compiler_params=_params(("parallel",)),
        name="conv_in",
    )(x2d, g, w_bf)


def _conv_kernel(gb_ref, cx_ref, u_ref, sa_ref, sb_ref, wa_ref, wb_ref, bias_ref, lng_ref, lnb_ref,
                 y_ref, na_ref, nb_ref, xa_s, xb_s, *, tt, rc):
    t = pl.program_id(1)

    @pl.when(t == 0)
    def _():
        xa_s[0:A_HALO, :] = sa_ref[0]
        xb_s[0:B_HALO, :] = sb_ref[0]

    @pl.when(t > 0)
    def _():
        xa_s[0:A_HALO, :] = xa_s[tt:tt + A_HALO, :]
        xb_s[0:B_HALO, :] = xb_s[tt:tt + B_HALO, :]

    xa_s[A_HALO:A_HALO + tt, :] = cx_ref[0]
    xb_s[B_HALO:B_HALO + tt, :] = u_ref[0]

    bias = bias_ref[...]
    lng = lng_ref[...]
    lnb = lnb_ref[...]
    a0 = A_HALO - (CONV_A_WIDTH - 1)
    b0 = B_HALO - (CONV_B_WIDTH - 1)
    for c in range(tt // rc):
        r0 = c * rc
        ya = wa_ref[0:1, :] * xa_s[a0 + r0:a0 + r0 + rc, :]
        for j in range(1, CONV_A_WIDTH):
            ya = ya + wa_ref[j:j + 1, :] * xa_s[a0 + j + r0:a0 + j + r0 + rc, :]
        y_ref[0, r0:r0 + rc, 0:D_A] = gb_ref[0, r0:r0 + rc, :] * ya

        yb = wb_ref[0:1, :] * xb_s[b0 + r0:b0 + r0 + rc, :]
        for j in range(1, CONV_B_WIDTH):
            yb = yb + wb_ref[j:j + 1, :] * xb_s[b0 + j + r0:b0 + j + r0 + rc, :]
        yb = yb + bias
        mu = jnp.mean(yb, axis=-1, keepdims=True)
        var = jnp.mean(jnp.square(yb - mu), axis=-1, keepdims=True)
        z = (yb - mu) * lax.rsqrt(var + EPS) * lng + lnb
        y_ref[0, r0:r0 + rc, D_A:D_A + D_B] = z * _sigmoid(z)

    na_ref[0] = xa_s[tt + A_HALO - (CONV_A_WIDTH - 1):tt + A_HALO, :]
    nb_ref[0] = xb_s[tt + B_HALO - (CONV_B_WIDTH - 1):tt + B_HALO, :]


def _conv(gb, cx, u, sa, sb, wa, wb, bias, lng, lnb, tt, rc):
    b, t, _ = cx.shape
    row = lambda i, j: (i, j, 0)
    per_b = lambda i, j: (i, 0, 0)
    return pl.pallas_call(
        functools.partial(_conv_kernel, tt=tt, rc=rc),
        grid=(b, t // tt),
        in_specs=[pl.BlockSpec((1, tt, D_A), row),
                  pl.BlockSpec((1, tt, D_A), row),
                  pl.BlockSpec((1, tt, D_B), row),
                  pl.BlockSpec((1, A_HALO, D_A), per_b),
                  pl.BlockSpec((1, B_HALO, D_B), per_b),
                  pl.BlockSpec((CONV_A_WIDTH, D_A), lambda i, j: (0, 0)),
                  pl.BlockSpec((CONV_B_WIDTH, D_B), lambda i, j: (0, 0)),
                  pl.BlockSpec((1, D_B), lambda i, j: (0, 0)),
                  pl.BlockSpec((1, D_B), lambda i, j: (0, 0)),
                  pl.BlockSpec((1, D_B), lambda i, j: (0, 0))],
        out_specs=[pl.BlockSpec((1, tt, D_MODEL), row),
                   pl.BlockSpec((1, CONV_A_WIDTH - 1, D_A), per_b),
                   pl.BlockSpec((1, CONV_B_WIDTH - 1, D_B), per_b)],
        out_shape=[jax.ShapeDtypeStruct((b, t, D_MODEL), F32),
                   jax.ShapeDtypeStruct((b, CONV_A_WIDTH - 1, D_A), F32),
                   jax.ShapeDtypeStruct((b, CONV_B_WIDTH - 1, D_B), F32)],
        scratch_shapes=[pltpu.VMEM((A_HALO + tt, D_A), F32),
                        pltpu.VMEM((B_HALO + tt, D_B), F32)],
        compiler_params=_params(("parallel", "arbitrary")),
        name="conv_mix",
    )(gb, cx, u, sa, sb, wa, wb, bias, lng, lnb)


def _out_mlp_kernel(a_ref, res_ref, wo_ref, g_ref, wu_ref, wd_ref, gf_ref, o_ref, *, final_norm, ff_chunk):
    x1 = res_ref[...] + jnp.dot(a_ref[...].astype(BF16), wo_ref[...], preferred_element_type=F32)
    xn = _rms_bf16(x1, g_ref[...])
    acc = x1
    for c in range(D_FF // ff_chunk):
        h = jnp.dot(xn, wu_ref[:, c * ff_chunk:(c + 1) * ff_chunk], preferred_element_type=F32)
        h = jnp.square(jnp.maximum(h, 0.0)).astype(BF16)
        acc = acc + jnp.dot(h, wd_ref[c * ff_chunk:(c + 1) * ff_chunk, :], preferred_element_type=F32)
    if final_norm:
        ms = jnp.mean(acc * acc, axis=-1, keepdims=True)
        acc = (acc * lax.rsqrt(ms + EPS)) * gf_ref[...]
    o_ref[...] = acc


def _out_mlp(a2d, res2d, wo_bf, g, wu_bf, wd_bf, gf, tm, final_norm):
    m = res2d.shape[0]
    row = lambda i: (i, 0)
    return pl.pallas_call(
        functools.partial(_out_mlp_kernel, final_norm=final_norm, ff_chunk=1024),
        grid=(m // tm,),
        in_specs=[pl.BlockSpec((tm, D_MODEL), row),
                  pl.BlockSpec((tm, D_MODEL), row),
                  _const_spec((D_MODEL, D_MODEL)),
                  _const_spec((1, D_MODEL)),
                  _const_spec((D_MODEL, D_FF)),
                  _const_spec((D_FF, D_MODEL)),
                  _const_spec((1, D_MODEL))],
        out_specs=pl.BlockSpec((tm, D_MODEL), row),
        out_shape=jax.ShapeDtypeStruct((m, D_MODEL), F32),
        compiler_params=_params(("parallel",)),
        name="out_mlp",
    )(a2d, res2d, wo_bf, g, wu_bf, wd_bf, gf)


def _attn_in_kernel(*refs, prompt, tm):
    if prompt:
        (x_ref, g_ref, wqkv_ref, wvt_ref, wqi_ref, wkw_ref,
         qb_ref, k_ref, v_ref, qib_ref, kw_ref, kb_ref, vt_ref) = refs
    else:
        x_ref, g_ref, wqkv_ref, wqi_ref, wkw_ref, qb_ref, k_ref, v_ref, qib_ref, kw_ref = refs
    xn = _rms_bf16(x_ref[...], g_ref[...])
    q = jnp.dot(xn, wqkv_ref[:, 0:D_MODEL], preferred_element_type=F32)
    qb_ref[...] = (q * QK_SCALE if prompt else q).astype(BF16)
    k = jnp.dot(xn, wqkv_ref[:, D_MODEL:2 * D_MODEL], preferred_element_type=F32)
    v = jnp.dot(xn, wqkv_ref[:, 2 * D_MODEL:3 * D_MODEL], preferred_element_type=F32)
    for h in range(N_HEADS):
        k_ref[:, h, :] = k[:, h * HEAD_DIM:(h + 1) * HEAD_DIM]
        v_ref[:, h, :] = v[:, h * HEAD_DIM:(h + 1) * HEAD_DIM]
    qib_ref[...] = jnp.dot(xn, wqi_ref[...], preferred_element_type=F32).astype(BF16)
    kw_ref[...] = jnp.dot(xn, wkw_ref[...], preferred_element_type=F32)
    if prompt:
        kb_ref[...] = k.astype(BF16)
        vt = lax.dot_general(wvt_ref[...], xn, (((1,), (1,)), ((), ())),
                             preferred_element_type=F32).astype(BF16)
        for s in range(tm // TK):
            vt_ref[0, s] = vt[:, s * TK:(s + 1) * TK]


def _attn_in(x2d, g, wqkv_bf, wvt_bf, wqi_bf, wkw_bf, tm, seq_len, prompt):
    m = x2d.shape[0]
    row = lambda i: (i, 0)
    row3 = lambda i: (i, 0, 0)
    nqi = N_IDX_HEADS * IDX_DIM
    in_specs = [pl.BlockSpec((tm, D_MODEL), row),
                _const_spec((1, D_MODEL)),
                _const_spec((D_MODEL, 3 * D_MODEL))]
    args = [x2d, g, wqkv_bf]
    if prompt:
        in_specs.append(_const_spec((D_MODEL, D_MODEL)))
        args.append(wvt_bf)
    in_specs += [_const_spec((D_MODEL, nqi)), _const_spec((D_MODEL, LANES))]
    args += [wqi_bf, wkw_bf]
    out_specs = [pl.BlockSpec((tm, D_MODEL), row),
                 pl.BlockSpec((tm, N_HEADS, HEAD_DIM), row3),
                 pl.BlockSpec((tm, N_HEADS, HEAD_DIM), row3),
                 pl.BlockSpec((tm, nqi), row),
                 pl.BlockSpec((tm, LANES), row)]
    out_shape = [jax.ShapeDtypeStruct((m, D_MODEL), BF16),
                 jax.ShapeDtypeStruct((m, N_HEADS, HEAD_DIM), F32),
                 jax.ShapeDtypeStruct((m, N_HEADS, HEAD_DIM), F32),
                 jax.ShapeDtypeStruct((m, nqi), BF16),
                 jax.ShapeDtypeStruct((m, LANES), F32)]
    if prompt:
        steps_per_seq = seq_len // tm
        out_specs += [pl.BlockSpec((tm, D_MODEL), row),
                      pl.BlockSpec((1, tm // TK, D_MODEL, TK),
                                   lambda i: (i // steps_per_seq, i % steps_per_seq, 0, 0))]
        out_shape += [jax.ShapeDtypeStruct((m, D_MODEL), BF16),
                      jax.ShapeDtypeStruct((m // seq_len, seq_len // TK, D_MODEL, TK), BF16)]
    return pl.pallas_call(
        functools.partial(_attn_in_kernel, prompt=prompt, tm=tm),
        grid=(m // tm,),
        in_specs=in_specs,
        out_specs=out_specs,
        out_shape=out_shape,
        compiler_params=_params(("parallel",)),
        name="attn_in",
    )(*args)


def _t5_bucket(d):
    n = jnp.maximum(d, 0)
    max_exact = NUM_BUCKETS // 2
    nf = jnp.maximum(n, 1).astype(F32)
    large = max_exact + (jnp.log(nf / max_exact) / math.log(MAX_DISTANCE / max_exact)
                         * (NUM_BUCKETS - max_exact)).astype(I32)
    large = jnp.minimum(large, NUM_BUCKETS - 1)
    return jnp.where(n < max_exact, n, large)


def _bias_of_bucket(bucket, relb_ref, h):
    val = jnp.full(bucket.shape, relb_ref[NUM_BUCKETS - 1, h], F32)
    for bk in range(NUM_BUCKETS - 2, -1, -1):
        val = jnp.where(bucket == bk, relb_ref[bk, h], val)
    return val


def _prompt_bias_kernel(relb_ref, o_ref):
    h = pl.program_id(0)
    c = lax.broadcasted_iota(I32, (NEAR_TILES * TK, TQ), 0)
    r = lax.broadcasted_iota(I32, (NEAR_TILES * TK, TQ), 1)
    o_ref[0] = _bias_of_bucket(_t5_bucket(r + TK - c), relb_ref, h) * LOG2E


def _prompt_bias(rel_bias):
    return pl.pallas_call(
        _prompt_bias_kernel,
        grid=(N_HEADS,),
        in_specs=[pl.BlockSpec(memory_space=pltpu.SMEM)],
        out_specs=pl.BlockSpec((1, NEAR_TILES * TK, TQ), lambda h: (h, 0, 0)),
        out_shape=jax.ShapeDtypeStruct((N_HEADS, NEAR_TILES * TK, TQ), F32),
        compiler_params=_params(("parallel",)),
        name="prompt_bias",
    )(rel_bias)


def _sample_bias_kernel(relb_ref, o_ref, *, t, past_len):
    pg = pl.program_id(0)
    q = lax.broadcasted_iota(I32, (t, PAGE_ROWS), 0)
    lane = lax.broadcasted_iota(I32, (t, PAGE_ROWS), 1)
    bucket = _t5_bucket(past_len + q - (pg * PAGE + (lane >> 3)))
    head = lane & (N_HEADS - 1)
    val = _bias_of_bucket(bucket, relb_ref, 0)
    for h in range(1, N_HEADS):
        val = jnp.where(head == h, _bias_of_bucket(bucket, relb_ref, h), val)
    o_ref[0] = val


def _sample_bias(rel_bias, t, past_len):
    n_pages = past_len // PAGE + 1
    return pl.pallas_call(
        functools.partial(_sample_bias_kernel, t=t, past_len=past_len),
        grid=(n_pages,),
        in_specs=[pl.BlockSpec(memory_space=pltpu.SMEM)],
        out_specs=pl.BlockSpec((1, t, PAGE_ROWS), lambda j: (j, 0, 0)),
        out_shape=jax.ShapeDtypeStruct((n_pages, t, PAGE_ROWS), F32),
        compiler_params=_params(("parallel",)),
        name="sample_bias",
    )(rel_bias)


def _sortable(score):
    bits = pltpu.bitcast(score, I32)
    return jnp.where(bits < 0, bits ^ INT_MAX, bits)


def _bisect(count_fn, n_bits, init, shape):
    def body(it, cur):
        trial = cur ^ lax.shift_left(jnp.int32(1), jnp.asarray(n_bits - 1 - it, I32))
        return jnp.where(count_fn(trial), trial, cur)
    return lax.fori_loop(0, n_bits, body, jnp.full(shape, init, I32))


def _prompt_attn_kernel(relb_ref, q_ref, qi_ref, kwq_ref, kw_ref, k_ref, vt_ref, db_ref, o_ref,
                        keys_s, acc_s, m_s, l_s, *, topk):
    i = pl.program_id(1)
    tiles_per_q = TQ // TK
    nt = (i + 1) * tiles_per_q
    nt_dims = (((1,), (1,)), ((), ()))
    row = (1, TQ)

    qi = qi_ref[0]
    qi_h = [qi[:, h * IDX_DIM:(h + 1) * IDX_DIM] for h in range(N_IDX_HEADS)]
    kwq_t = kwq_ref[0].T
    w_scale = (N_IDX_HEADS ** -0.5) * (IDX_DIM ** -0.5)
    w_h = [kwq_t[IDX_DIM + h:IDX_DIM + h + 1, :] * w_scale for h in range(N_IDX_HEADS)]
    krow = lax.broadcasted_iota(I32, (TK, TQ), 0)
    qpos = i * TQ + lax.broadcasted_iota(I32, (TK, TQ), 1)

    def score_chunk(jj, _):
        r0 = pl.multiple_of(jj * TQ, TQ)
        ki_t = kw_ref[0, pl.ds(r0, TQ), 0:IDX_DIM].astype(BF16)
        score = jnp.zeros((TQ, TQ), F32)
        for h in range(N_IDX_HEADS):
            s = lax.dot_general(ki_t, qi_h[h], nt_dims, preferred_element_type=F32)
            score = score + jnp.maximum(s, 0.0) * w_h[h]
        key = _sortable(score)
        for s in range(tiles_per_q):
            j = jj * tiles_per_q + s
            keys_s[j] = jnp.where(j * TK + krow <= qpos, key[s * TK:(s + 1) * TK, :], INT_MIN)
        return 0

    lax.fori_loop(0, i + 1, score_chunk, 0)

    def count(pred_fn):
        def body(j, c):
            m = jnp.where(pred_fn(keys_s[j], j), 1, 0).astype(I32)
            return c + jnp.sum(m.reshape(TK // 8, 8, TQ), axis=0)
        c = lax.fori_loop(0, nt, body, jnp.zeros((8, TQ), I32))
        return jnp.sum(c, axis=0, keepdims=True)

    thr = _bisect(lambda x: count(lambda kt, j: kt >= x) >= topk, 32, INT_MIN, row)
    n_gt = count(lambda kt, j: kt > thr)
    thr1 = jnp.maximum(thr, INT_MIN + 1)
    n_eq = count(lambda kt, j: kt == thr1)
    need = topk - n_gt

    def tie_bound():
        return _bisect(lambda x: count(lambda kt, j: jnp.where(kt == thr, j * TK + krow, INT_MAX) < x) < need,
                       31, 0, row)

    has_tie = jnp.max(n_eq - need) > 0
    xb = lax.cond(has_tie, tie_bound, lambda: jnp.full(row, INT_MAX, I32))

    q = q_ref[0]
    q_h = [q[:, h * HEAD_DIM:(h + 1) * HEAD_DIM] for h in range(N_HEADS)]
    acc_s[...] = jnp.zeros_like(acc_s)
    m_s[...] = jnp.full_like(m_s, NEG_BIG)
    l_s[...] = jnp.zeros_like(l_s)

    def attend(j, near):
        r0 = pl.multiple_of(j * TK, TK)
        kt = keys_s[j]
        tied = jnp.where(j * TK + krow <= xb, 0.0, -jnp.inf)
        amask = jnp.where(kt > thr1, 0.0, jnp.where(kt == thr1, tied, -jnp.inf))
        for h in range(N_HEADS):
            hs = slice(h * HEAD_DIM, (h + 1) * HEAD_DIM)
            k_t = k_ref[0, pl.ds(r0, TK), hs]
            s = lax.dot_general(k_t, q_h[h], nt_dims, preferred_element_type=F32)
            m_old = m_s[h:h + 1, :]
            if near is None:
                c = relb_ref[NUM_BUCKETS - 1, h] * LOG2E
                lg = s + amask
                m_new = jnp.maximum(m_old, jnp.max(lg, axis=0, keepdims=True) + c)
                p = jnp.exp2(lg - (m_new - c))
            else:
                lg = s + (amask + db_ref[h, near * TK:(near + 1) * TK, :])
                m_new = jnp.maximum(m_old, jnp.max(lg, axis=0, keepdims=True))
                p = jnp.exp2(lg - m_new)
            alpha = jnp.exp2(m_old - m_new)
            l_s[h:h + 1, :] = alpha * l_s[h:h + 1, :] + jnp.sum(p, axis=0, keepdims=True)
            m_s[h:h + 1, :] = m_new
            pv = jnp.dot(vt_ref[0, j, hs, :], p.astype(BF16), preferred_element_type=F32)
            acc_s[hs, :] = alpha * acc_s[hs, :] + pv

    first_near = i * tiles_per_q - 1

    def far_tile(j, _):
        attend(j, None)
        return 0

    lax.fori_loop(0, first_near, far_tile, 0)

    @pl.when(i >= 1)
    def _():
        attend(first_near, 0)

    for near in range(1, NEAR_TILES):
        attend(first_near + near, near)

    for h in range(N_HEADS):
        hs = slice(h * HEAD_DIM, (h + 1) * HEAD_DIM)
        o_ref[0, :, hs] = (acc_s[hs, :] / l_s[h:h + 1, :]).T.astype(BF16)


def _prompt_attn(rel_bias, qb, qib, kw, kb, vt, dbias, topk):
    b, t, _ = qb.shape
    nqi = N_IDX_HEADS * IDX_DIM
    assert TK >= 113 and TQ % TK == 0
    return pl.pallas_call(
        functools.partial(_prompt_attn_kernel, topk=topk),
        grid=(b, t // TQ),
        in_specs=[pl.BlockSpec(memory_space=pltpu.SMEM),
                  pl.BlockSpec((1, TQ, D_MODEL), lambda bi, i: (bi, i, 0)),
                  pl.BlockSpec((1, TQ, nqi), lambda bi, i: (bi, i, 0)),
                  pl.BlockSpec((1, TQ, LANES), lambda bi, i: (bi, i, 0)),
                  pl.BlockSpec((1, t, LANES), lambda bi, i: (bi, 0, 0)),
                  pl.BlockSpec((1, t, D_MODEL), lambda bi, i: (bi, 0, 0)),
                  pl.BlockSpec((1, t // TK, D_MODEL, TK), lambda bi, i: (bi, 0, 0, 0)),
                  _const_spec((N_HEADS, NEAR_TILES * TK, TQ))],
        out_specs=pl.BlockSpec((1, TQ, D_MODEL), lambda bi, i: (bi, i, 0)),
        out_shape=jax.ShapeDtypeStruct((b, t, D_MODEL), BF16),
        scratch_shapes=[pltpu.VMEM((t // TK, TK, TQ), I32),
                        pltpu.VMEM((D_MODEL, TQ), F32),
                        pltpu.VMEM((N_HEADS, TQ), F32),
                        pltpu.VMEM((N_HEADS, TQ), F32)],
        compiler_params=_params(("parallel", "arbitrary")),
        name="prompt_attn",
    )(rel_bias, qb, qib, kw, kw, kb, vt, dbias)


def _sample_attn_kernel(pt_ref, q_ref, qi_ref, kw_ref, kn_ref, vn_ref, sb_ref, cki_hbm, ck_hbm, cv_hbm,
                        o_ref, ibuf, kbuf, vbuf, keys_s, lg_s, isem, ksem, vsem,
                        *, t, past_len, topk, page_base):
    b = pl.program_id(0)
    n_idx_chunks = past_len // IDX_CHUNK
    n_pages = past_len // PAGE
    n_kv_chunks = n_pages // KV_PAGES
    n_tiles = n_pages + 1
    rows = N_HEADS * t
    scale = HEAD_DIM ** -0.5
    nt_dims = (((1,), (1,)), ((), ()))

    def idx_copy(c, slot, p):
        page = page_base + pt_ref[b, c * IDX_PAGES + p]
        return pltpu.make_async_copy(cki_hbm.at[page], ibuf.at[slot, p], isem.at[slot])

    def k_copy(c, slot, p):
        page = page_base + pt_ref[b, c * KV_PAGES + p]
        return pltpu.make_async_copy(ck_hbm.at[page], kbuf.at[slot, pl.ds(p * PAGE_ROWS, PAGE_ROWS)],
                                     ksem.at[slot])

    def v_copy(c, slot, p):
        page = page_base + pt_ref[b, c * KV_PAGES + p]
        return pltpu.make_async_copy(cv_hbm.at[page], vbuf.at[slot, pl.ds(p * PAGE_ROWS, PAGE_ROWS)],
                                     vsem.at[slot])

    def start(copy_fn, n, c, slot):
        for p in range(n):
            copy_fn(c, slot, p).start()

    def wait(copy_fn, n, c, slot):
        for p in range(n):
            copy_fn(c, slot, p).wait()

    start(idx_copy, IDX_PAGES, 0, 0)
    start(k_copy, KV_PAGES, 0, 0)
    start(v_copy, KV_PAGES, 0, 0)

    qi = qi_ref[0].astype(F32)
    qi_rows = jnp.concatenate([qi[:, h * IDX_DIM:(h + 1) * IDX_DIM] for h in range(N_IDX_HEADS)],
                              axis=0).astype(BF16)
    w = kw_ref[0][:, IDX_DIM:IDX_DIM + N_IDX_HEADS] * (N_IDX_HEADS ** -0.5) * (IDX_DIM ** -0.5)
    w_h = [w[:, h:h + 1] for h in range(N_IDX_HEADS)]

    def scores(ki_bf, transposed):
        if transposed:
            s = jnp.dot(qi_rows, ki_bf, preferred_element_type=F32)
        else:
            s = lax.dot_general(qi_rows, ki_bf, nt_dims, preferred_element_type=F32)
        score = jnp.maximum(s[0:t], 0.0) * w_h[0]
        for h in range(1, N_IDX_HEADS):
            score = score + jnp.maximum(s[h * t:(h + 1) * t], 0.0) * w_h[h]
        return score

    def idx_chunk(c, _):
        slot = c % 2

        @pl.when(c + 1 < n_idx_chunks)
        def _():
            start(idx_copy, IDX_PAGES, c + 1, 1 - slot)

        wait(idx_copy, IDX_PAGES, c, slot)
        ki_t = jnp.concatenate([ibuf[slot, p] for p in range(IDX_PAGES)], axis=1)
        key = _sortable(scores(ki_t.astype(BF16), True))
        for k in range(IDX_PAGES):
            keys_s[c * IDX_PAGES + k] = key[:, k * PAGE:(k + 1) * PAGE]
        return 0

    lax.fori_loop(0, n_idx_chunks, idx_chunk, 0)

    ki_new = jnp.concatenate([kw_ref[0][:, 0:IDX_DIM], jnp.zeros((PAGE - t, IDX_DIM), F32)], axis=0)
    qrow = lax.broadcasted_iota(I32, (t, PAGE), 0)
    lane = lax.broadcasted_iota(I32, (t, PAGE), 1)
    keys_s[n_tiles - 1] = jnp.where(lane <= qrow, _sortable(scores(ki_new.astype(BF16), False)), INT_MIN)

    pos = (lax.broadcasted_iota(I32, (n_tiles, t, PAGE), 0) * PAGE
           + lax.broadcasted_iota(I32, (n_tiles, t, PAGE), 2))

    def count(pred):
        m = jnp.where(pred, 1, 0).astype(I32)
        part = m[0:n_pages]
        for group in (4, 4):
            part = jnp.sum(part.reshape(group, part.shape[0] // group, t, PAGE), axis=0)
        per_lane = jnp.sum(part, axis=0) + m[n_pages]
        return jnp.sum(per_lane, axis=1, keepdims=True)

    thr = _bisect(lambda x: count(keys_s[...] >= x[None]) >= topk, 32, INT_MIN, (t, 1))
    kall = keys_s[...]
    need = topk - count(kall > thr[None])
    thr1 = jnp.maximum(thr, INT_MIN + 1)[None]

    def tie_bound():
        return _bisect(lambda x: count((keys_s[...] == thr[None]) & (pos < x[None])) < need, 31, 0, (t, 1))

    has_tie = jnp.max(count(kall == thr1) - need) > 0
    xb = lax.cond(has_tie, tie_bound, lambda: jnp.full((t, 1), INT_MAX, I32))
    tied = jnp.where(pos <= xb[None], 0.0, NEG_BIG)
    keys_s[...] = pltpu.bitcast(
        jnp.where(kall > thr1, 0.0, jnp.where(kall == thr1, tied, NEG_BIG)).astype(F32), I32)

    q = q_ref[0].astype(F32)
    q_rows = jnp.concatenate([q[:, h * HEAD_DIM:(h + 1) * HEAD_DIM] for h in range(N_HEADS)],
                             axis=0).astype(BF16)

    def head_of_lane(n):
        return lax.broadcasted_iota(I32, (t, n), 1) & (N_HEADS - 1)

    def expand_mask(tile_bits, n):
        am = pltpu.bitcast(tile_bits, F32)
        key_of_lane = lax.broadcasted_iota(I32, (t, LANES), 1) >> 3
        return jnp.concatenate([jnp.take_along_axis(am, c * (LANES // N_HEADS) + key_of_lane, axis=1)
                                for c in range(n // LANES)], axis=1)

    def page_logits(x_bf, bias, tile_bits):
        n = x_bf.shape[0]
        r = lax.dot_general(q_rows, x_bf, nt_dims, preferred_element_type=F32)
        head = head_of_lane(n)
        lg = r[0:t]
        for h in range(1, N_HEADS):
            lg = jnp.where(head == h, r[h * t:(h + 1) * t], lg)
        return lg * scale + bias + expand_mask(tile_bits, n)

    def k_chunk(c, m_run):
        slot = c % 2

        @pl.when(c + 1 < n_kv_chunks)
        def _():
            start(k_copy, KV_PAGES, c + 1, 1 - slot)

        wait(k_copy, KV_PAGES, c, slot)
        for p in range(KV_PAGES):
            pg = c * KV_PAGES + p
            x_bf = kbuf[slot, p * PAGE_ROWS:(p + 1) * PAGE_ROWS, :].astype(BF16)
            lg = page_logits(x_bf, sb_ref[pg], keys_s[pg])
            lg_s[pg] = lg
            for s in range(PAGE_ROWS // LANES):
                m_run = jnp.maximum(m_run, lg[:, s * LANES:(s + 1) * LANES])
        return m_run

    m_run = lax.fori_loop(0, n_kv_chunks, k_chunk, jnp.full((t, LANES), NEG_BIG, F32))

    new_rows = t * N_HEADS
    k_new = jnp.concatenate([kn_ref[0], jnp.zeros((LANES - new_rows, HEAD_DIM), F32)], axis=0).astype(BF16)
    lg_new = page_logits(k_new, sb_ref[n_pages][:, 0:LANES], keys_s[n_tiles - 1])
    lg_new = jnp.where(lax.broadcasted_iota(I32, (t, LANES), 1) < new_rows, lg_new, NEG_BIG)
    m_run = jnp.maximum(m_run, lg_new)

    def per_head_allreduce(x, op):
        for shift in (8, 16, 32, 64):
            x = op(x, pltpu.roll(x, shift, axis=1))
        return x

    m_fin = per_head_allreduce(m_run, jnp.maximum)
    m_wide = jnp.concatenate([m_fin] * (PAGE_ROWS // LANES), axis=1)

    def weighted(p, x_bf):
        head = head_of_lane(p.shape[1])
        p_rows = jnp.concatenate([jnp.where(head == h, p, 0.0) for h in range(N_HEADS)],
                                 axis=0).astype(BF16)
        return jnp.dot(p_rows, x_bf, preferred_element_type=F32)

    def v_chunk(c, carry):
        l_run, acc = carry
        slot = c % 2

        @pl.when(c + 1 < n_kv_chunks)
        def _():
            start(v_copy, KV_PAGES, c + 1, 1 - slot)

        wait(v_copy, KV_PAGES, c, slot)
        for p in range(KV_PAGES):
            pg = c * KV_PAGES + p
            pr = jnp.exp(lg_s[pg] - m_wide)
            for s in range(PAGE_ROWS // LANES):
                l_run = l_run + pr[:, s * LANES:(s + 1) * LANES]
            x_bf = vbuf[slot, p * PAGE_ROWS:(p + 1) * PAGE_ROWS, :].astype(BF16)
            acc = acc + weighted(pr, x_bf)
        return l_run, acc

    l_run, acc = lax.fori_loop(0, n_kv_chunks, v_chunk,
                               (jnp.zeros((t, LANES), F32), jnp.zeros((rows, HEAD_DIM), F32)))

    p_new = jnp.exp(lg_new - m_fin)
    v_new = jnp.concatenate([vn_ref[0], jnp.zeros((LANES - new_rows, HEAD_DIM), F32)], axis=0).astype(BF16)
    l_fin = per_head_allreduce(l_run + p_new, jnp.add)
    acc = acc + weighted(p_new, v_new)
    for h in range(N_HEADS):
        o_ref[0, :, h * HEAD_DIM:(h + 1) * HEAD_DIM] = (acc[h * t:(h + 1) * t, :] / l_fin[:, h:h + 1]).astype(BF16)


def _sample_attn(page_table, qb, qib, kw, k_new, v_new, sbias, cki, ck, cv, topk, page_base):
    b, t, _ = qb.shape
    n_pages = page_table.shape[1]
    past_len = n_pages * PAGE
    nqi = N_IDX_HEADS * IDX_DIM
    assert t * N_HEADS <= LANES and n_pages % KV_PAGES == 0 and n_pages % IDX_PAGES == 0
    per_b = lambda bi, pt: (bi, 0, 0)
    grid_spec = pltpu.PrefetchScalarGridSpec(
        num_scalar_prefetch=1,
        grid=(b,),
        in_specs=[pl.BlockSpec((1, t, D_MODEL), per_b),
                  pl.BlockSpec((1, t, nqi), per_b),
                  pl.BlockSpec((1, t, LANES), per_b),
                  pl.BlockSpec((1, t * N_HEADS, HEAD_DIM), per_b),
                  pl.BlockSpec((1, t * N_HEADS, HEAD_DIM), per_b),
                  pl.BlockSpec(sbias.shape, lambda bi, pt: (0, 0, 0), pipeline_mode=pl.Buffered(1)),
                  pl.BlockSpec(memory_space=pl.ANY),
                  pl.BlockSpec(memory_space=pl.ANY),
                  pl.BlockSpec(memory_space=pl.ANY)],
        out_specs=pl.BlockSpec((1, t, D_MODEL), per_b),
        scratch_shapes=[pltpu.VMEM((2, IDX_PAGES, IDX_DIM, PAGE), F32),
                        pltpu.VMEM((2, KV_PAGES * PAGE_ROWS, HEAD_DIM), F32),
                        pltpu.VMEM((2, KV_PAGES * PAGE_ROWS, HEAD_DIM), F32),
                        pltpu.VMEM((n_pages + 1, t, PAGE), I32),
                        pltpu.VMEM((n_pages, t, PAGE_ROWS), F32),
                        pltpu.SemaphoreType.DMA((2,)),
                        pltpu.SemaphoreType.DMA((2,)),
                        pltpu.SemaphoreType.DMA((2,))])
    return pl.pallas_call(
        functools.partial(_sample_attn_kernel, t=t, past_len=past_len, topk=topk, page_base=page_base),
        grid_spec=grid_spec,
        out_shape=jax.ShapeDtypeStruct((b, t, D_MODEL), BF16),
        compiler_params=_params(("arbitrary",)),
        name="sample_attn",
    )(page_table, qb, qib, kw, k_new, v_new, sbias, cki, ck, cv)


def kernel(x_prompt, x_sample, state_conv_a, state_conv_b, cache_k, cache_v, cache_kidx, page_table,
           norm_conv, w_in_conv, conv_a_w, conv_b_w, conv_b_bias, ln_b_gain, ln_b_bias, w_out_conv,
           norm_attn, w_in_attn, w_out_attn, rel_bias, norm_mlp, w_up, w_down, norm_final):
    depth = norm_mlp.shape[0]
    n_pool = cache_k.shape[1]
    past_len = page_table.shape[1] * PAGE
    nqkv = 3 * D_MODEL
    nqi = N_IDX_HEADS * IDX_DIM
    row = lambda a: a.reshape(1, -1)

    cki_flat = jnp.swapaxes(cache_kidx, 2, 3).reshape(-1, IDX_DIM, PAGE)
    ck_flat = cache_k.reshape(-1, PAGE_ROWS, HEAD_DIM)
    cv_flat = cache_v.reshape(-1, PAGE_ROWS, HEAD_DIM)
    dbias = _prompt_bias(rel_bias)
    sbias = _sample_bias(rel_bias, x_sample.shape[1], past_len)

    def run(x, sample):
        b, t, _ = x.shape
        m = b * t
        tm = min(512, m)
        x2d = x.reshape(m, D_MODEL)
        new_a, new_b, new_k, new_v, new_ki = [], [], [], [], []
        for layer in range(depth):
            i = layer // 2
            last = layer == depth - 1
            if layer % 2 == 0:
                if sample:
                    buf_a, buf_b = state_conv_a[i], state_conv_b[i]
                else:
                    buf_a = jnp.zeros((b, CONV_A_WIDTH - 1, D_A), F32)
                    buf_b = jnp.zeros((b, CONV_B_WIDTH - 1, D_B), F32)
                sa = jnp.pad(buf_a, ((0, 0), (A_HALO - (CONV_A_WIDTH - 1), 0), (0, 0)))
                sb = jnp.pad(buf_b, ((0, 0), (B_HALO - (CONV_B_WIDTH - 1), 0), (0, 0)))
                gb, cx, u = _conv_in(x2d, row(norm_conv[i]), w_in_conv[i].astype(BF16), tm)
                tt = min(512, t)
                y, na, nb = _conv(gb.reshape(b, t, D_A), cx.reshape(b, t, D_A), u.reshape(b, t, D_B), sa, sb,
                                  conv_a_w[i], conv_b_w[i], row(conv_b_bias[i]), row(ln_b_gain[i]),
                                  row(ln_b_bias[i]), tt, min(64, tt))
                new_a.append(na)
                new_b.append(nb)
                mix = y.reshape(m, D_MODEL)
                w_out = w_out_conv[i]
            else:
                w_in = w_in_attn[i]
                wkw = jnp.pad(w_in[:, nqkv + nqi:], ((0, 0), (0, LANES - IDX_DIM - N_IDX_HEADS)))
                wvt = w_in[:, 2 * D_MODEL:nqkv].T.astype(BF16)
                outs = _attn_in(x2d, row(norm_attn[i]), w_in[:, :nqkv].astype(BF16), wvt,
                                w_in[:, nqkv:nqkv + nqi].astype(BF16), wkw.astype(BF16), tm, t, not sample)
                qb, k, v, qib, kw = outs[:5]
                shp = lambda a: a.reshape(b, t, a.shape[-1])
                if sample:
                    topk = min(TOPK_MAX, (past_len + t) // 4)
                    o = _sample_attn(page_table, shp(qb), shp(qib), shp(kw),
                                     k.reshape(b, t * N_HEADS, HEAD_DIM), v.reshape(b, t * N_HEADS, HEAD_DIM),
                                     sbias, cki_flat, ck_flat, cv_flat, topk, i * n_pool)
                else:
                    topk = min(TOPK_MAX, t // 4)
                    kb, vt = outs[5:]
                    o = _prompt_attn(rel_bias, shp(qb), shp(qib), shp(kw), shp(kb), vt, dbias, topk)
                new_k.append(k.reshape(b, t, N_HEADS, HEAD_DIM))
                new_v.append(v.reshape(b, t, N_HEADS, HEAD_DIM))
                new_ki.append(kw[:, :IDX_DIM].reshape(b, t, IDX_DIM))
                mix = o.reshape(m, D_MODEL)
                w_out = w_out_attn[i]
            x2d = _out_mlp(mix, x2d, w_out.astype(BF16), row(norm_mlp[layer]), w_up[layer].astype(BF16),
                           w_down[layer].astype(BF16), row(norm_final), tm, last)
        return (x2d.reshape(b, t, D_MODEL), jnp.stack(new_a), jnp.stack(new_b),
                jnp.stack(new_k), jnp.stack(new_v), jnp.stack(new_ki))

    y_p, ca_p, cb_p, k_p, v_p, ki_p = run(x_prompt, False)
    y_s, ca_s, cb_s, k_s, v_s, ki_s = run(x_sample, True)
    return (y_p, y_s, ca_p, cb_p, k_p, v_p, ki_p, ca_s, cb_s, k_s, v_s, ki_s)
```

```python
import functools
import math

import jax
import jax.numpy as jnp
from jax import lax
from jax.experimental import pallas as pl
from jax.experimental.pallas import tpu as pltpu

F32 = jnp.float32
BF16 = jnp.bfloat16
I32 = jnp.int32

D_MODEL = 1024
D_A = 512
D_B = 512
CONV_A_WIDTH = 3
CONV_B_WIDTH = 31
N_HEADS = 8
HEAD_DIM = 128
N_IDX_HEADS = 8
IDX_DIM = 64
TOPK_MAX = 256
NUM_BUCKETS = 32
MAX_DISTANCE = 128
D_FF = 4096
EPS = 1e-6
PAGE = 128
LANES = 128
SUBLANES = 8

INT_MIN = -(2 ** 31)
INT_MAX = 2 ** 31 - 1
NEG_BIG = -1e30

VMEM_LIMIT_BYTES = 56 * 1024 * 1024

A_HALO = 8
B_HALO = 32

TQ = 256
TK = 128
NEAR_TILES = TQ // TK + 1
def _first_far_distance():
    exact = NUM_BUCKETS // 2
    d = exact
    while exact + int(math.log(d / exact) / math.log(MAX_DISTANCE / exact) * (NUM_BUCKETS - exact)) < NUM_BUCKETS - 1:
        d += 1
    return d


FAR_DISTANCE = _first_far_distance()
LOG2E = math.log2(math.e)
QK_SCALE = (HEAD_DIM ** -0.5) * LOG2E
KV_PAGES = 4
KV_SLOTS = 4
IDX_PAGES = 16
IDX_CHUNK = IDX_PAGES * PAGE
PAGE_ROWS = PAGE * N_HEADS


def _params(sem):
    return pltpu.CompilerParams(dimension_semantics=sem, vmem_limit_bytes=VMEM_LIMIT_BYTES)


def _const_spec(shape):
    nd = len(shape)
    return pl.BlockSpec(shape, lambda *_: (0,) * nd, pipeline_mode=pl.Buffered(1))


def _rms_bf16(x, g):
    ms = jnp.mean(x * x, axis=-1, keepdims=True)
    return ((x * lax.rsqrt(ms + EPS)) * g).astype(BF16)


def _sigmoid(x):
    return 1.0 / (1.0 + jnp.exp(-x))


def _conv_in_kernel(x_ref, g_ref, w_ref, gb_ref, cx_ref, u_ref):
    xn = _rms_bf16(x_ref[...], g_ref[...])
    proj = jnp.dot(xn, w_ref[...], preferred_element_type=F32)
    gb_ref[...] = proj[:, :D_A]
    cx_ref[...] = proj[:, D_A:2 * D_A] * proj[:, 2 * D_A:3 * D_A]
    u_ref[...] = proj[:, 3 * D_A:3 * D_A + D_B] * _sigmoid(proj[:, 3 * D_A + D_B:])


def _conv_in(x2d, g, w_bf, tm):
    m = x2d.shape[0]
    n = w_bf.shape[1]
    out = jax.ShapeDtypeStruct((m, D_A), F32)
    return pl.pallas_call(
        _conv_in_kernel,
        grid=(m // tm,),
        in_specs=[pl.BlockSpec((tm, D_MODEL), lambda i: (i, 0)),
                  _const_spec((1, D_MODEL)),
                  _const_spec((D_MODEL, n))],
        out_specs=[pl.BlockSpec((tm, D_A), lambda i: (i, 0))] * 3,
        out_shape=[out, out, out],
        compiler_params=_params(("parallel",)),
        name="conv_in",
    )(x2d, g, w_bf)


def _conv_kernel(gb_ref, cx_ref, u_ref, sa_ref, sb_ref, wa_ref, wb_ref, bias_ref, lng_ref, lnb_ref,
                 y_ref, na_ref, nb_ref, xa_s, xb_s, sh_s, *, tt, rc):
    t = pl.program_id(1)

    @pl.when(t == 0)
    def _():
        xa_s[0:A_HALO, :] = sa_ref[0]
        xb_s[0:B_HALO, :] = sb_ref[0]

    @pl.when(t > 0)
    def _():
        xa_s[0:A_HALO, :] = xa_s[tt:tt + A_HALO, :]
        xb_s[0:B_HALO, :] = xb_s[tt:tt + B_HALO, :]

    xa_s[A_HALO:A_HALO + tt, :] = cx_ref[0]
    xb_s[B_HALO:B_HALO + tt, :] = u_ref[0]

    bias = bias_ref[...]
    lng = lng_ref[...]
    lnb = lnb_ref[...]
    a0 = A_HALO - (CONV_A_WIDTH - 1)
    b0 = B_HALO - (CONV_B_WIDTH - 1)
    for c in range(tt // rc):
        r0 = c * rc
        ya = wa_ref[0:1, :] * xa_s[a0 + r0:a0 + r0 + rc, :]
        for j in range(1, CONV_A_WIDTH):
            ya = ya + wa_ref[j:j + 1, :] * xa_s[a0 + j + r0:a0 + j + r0 + rc, :]
        y_ref[0, r0:r0 + rc, 0:D_A] = gb_ref[0, r0:r0 + rc, :] * ya

        yb = None
        for shift in range(SUBLANES):
            offs = [o for o in range(b0, b0 + CONV_B_WIDTH) if o % SUBLANES == shift]
            if not offs:
                continue
            rows_needed = offs[-1] - shift + rc
            sh_s[shift, 0:rows_needed, :] = xb_s[r0 + shift:r0 + shift + rows_needed, :]
            for o in offs:
                term = wb_ref[o - b0:o - b0 + 1, :] * sh_s[shift, o - shift:o - shift + rc, :]
                yb = term if yb is None else yb + term
        yb = yb + bias
        mu = jnp.mean(yb, axis=-1, keepdims=True)
        var = jnp.mean(jnp.square(yb - mu), axis=-1, keepdims=True)
        z = (yb - mu) * lax.rsqrt(var + EPS) * lng + lnb
        y_ref[0, r0:r0 + rc, D_A:D_A + D_B] = z * _sigmoid(z)

    na_ref[0] = xa_s[tt + A_HALO - (CONV_A_WIDTH - 1):tt + A_HALO, :]
    nb_ref[0] = xb_s[tt + B_HALO - (CONV_B_WIDTH - 1):tt + B_HALO, :]


def _conv(gb, cx, u, sa, sb, wa, wb, bias, lng, lnb, tt, rc):
    b, t, _ = cx.shape
    row = lambda i, j: (i, j, 0)
    per_b = lambda i, j: (i, 0, 0)
    return pl.pallas_call(
        functools.partial(_conv_kernel, tt=tt, rc=rc),
        grid=(b, t // tt),
        in_specs=[pl.BlockSpec((1, tt, D_A), row),
                  pl.BlockSpec((1, tt, D_A), row),
                  pl.BlockSpec((1, tt, D_B), row),
                  pl.BlockSpec((1, A_HALO, D_A), per_b),
                  pl.BlockSpec((1, B_HALO, D_B), per_b),
                  pl.BlockSpec((CONV_A_WIDTH, D_A), lambda i, j: (0, 0)),
                  pl.BlockSpec((CONV_B_WIDTH, D_B), lambda i, j: (0, 0)),
                  pl.BlockSpec((1, D_B), lambda i, j: (0, 0)),
                  pl.BlockSpec((1, D_B), lambda i, j: (0, 0)),
                  pl.BlockSpec((1, D_B), lambda i, j: (0, 0))],
        out_specs=[pl.BlockSpec((1, tt, D_MODEL), row),
                   pl.BlockSpec((1, CONV_A_WIDTH - 1, D_A), per_b),
                   pl.BlockSpec((1, CONV_B_WIDTH - 1, D_B), per_b)],
        out_shape=[jax.ShapeDtypeStruct((b, t, D_MODEL), F32),
                   jax.ShapeDtypeStruct((b, CONV_A_WIDTH - 1, D_A), F32),
                   jax.ShapeDtypeStruct((b, CONV_B_WIDTH - 1, D_B), F32)],
        scratch_shapes=[pltpu.VMEM((A_HALO + tt, D_A), F32),
                        pltpu.VMEM((B_HALO + tt, D_B), F32),
                        pltpu.VMEM((SUBLANES, B_HALO + rc, D_B), F32)],
        compiler_params=_params(("parallel", "arbitrary")),
        name="conv_mix",
    )(gb, cx, u, sa, sb, wa, wb, bias, lng, lnb)


def _out_mlp_kernel(a_ref, res_ref, wo_ref, g_ref, wu_ref, wd_ref, gf_ref, o_ref, *, final_norm, ff_chunk):
    x1 = res_ref[...] + jnp.dot(a_ref[...].astype(BF16), wo_ref[...], preferred_element_type=F32)
    xn = _rms_bf16(x1, g_ref[...])
    acc = x1
    for c in range(D_FF // ff_chunk):
        h = jnp.dot(xn, wu_ref[:, c * ff_chunk:(c + 1) * ff_chunk], preferred_element_type=F32)
        h = jnp.square(jnp.maximum(h, 0.0)).astype(BF16)
        acc = acc + jnp.dot(h, wd_ref[c * ff_chunk:(c + 1) * ff_chunk, :], preferred_element_type=F32)
    if final_norm:
        ms = jnp.mean(acc * acc, axis=-1, keepdims=True)
        acc = (acc * lax.rsqrt(ms + EPS)) * gf_ref[...]
    o_ref[...] = acc


def _out_mlp(a2d, res2d, wo_bf, g, wu_bf, wd_bf, gf, tm, final_norm):
    m = res2d.shape[0]
    row = lambda i: (i, 0)
    return pl.pallas_call(
        functools.partial(_out_mlp_kernel, final_norm=final_norm, ff_chunk=1024),
        grid=(m // tm,),
        in_specs=[pl.BlockSpec((tm, D_MODEL), row),
                  pl.BlockSpec((tm, D_MODEL), row),
                  _const_spec((D_MODEL, D_MODEL)),
                  _const_spec((1, D_MODEL)),
                  _const_spec((D_MODEL, D_FF)),
                  _const_spec((D_FF, D_MODEL)),
                  _const_spec((1, D_MODEL))],
        out_specs=pl.BlockSpec((tm, D_MODEL), row),
        out_shape=jax.ShapeDtypeStruct((m, D_MODEL), F32),
        compiler_params=_params(("parallel",)),
        name="out_mlp",
    )(a2d, res2d, wo_bf, g, wu_bf, wd_bf, gf)


def _attn_in_kernel(*refs, prompt, tm):
    if prompt:
        (x_ref, g_ref, wqkv_ref, wvt_ref, wqi_ref, wkw_ref,
         qb_ref, k_ref, v_ref, qib_ref, kw_ref, kb_ref, vt_ref) = refs
    else:
        x_ref, g_ref, wqkv_ref, wqi_ref, wkw_ref, qb_ref, k_ref, v_ref, qib_ref, kw_ref = refs
    xn = _rms_bf16(x_ref[...], g_ref[...])
    q = jnp.dot(xn, wqkv_ref[:, 0:D_MODEL], preferred_element_type=F32)
    qb_ref[...] = (q * QK_SCALE if prompt else q).astype(BF16)
    k = jnp.dot(xn, wqkv_ref[:, D_MODEL:2 * D_MODEL], preferred_element_type=F32)
    v = jnp.dot(xn, wqkv_ref[:, 2 * D_MODEL:3 * D_MODEL], preferred_element_type=F32)
    for h in range(N_HEADS):
        k_ref[:, h, :] = k[:, h * HEAD_DIM:(h + 1) * HEAD_DIM]
        v_ref[:, h, :] = v[:, h * HEAD_DIM:(h + 1) * HEAD_DIM]
    qib_ref[...] = jnp.dot(xn, wqi_ref[...], preferred_element_type=F32).astype(BF16)
    kw_ref[...] = jnp.dot(xn, wkw_ref[...], preferred_element_type=F32)
    if prompt:
        kb_ref[...] = k.astype(BF16)
        vt = lax.dot_general(wvt_ref[...], xn, (((1,), (1,)), ((), ())),
                             preferred_element_type=F32).astype(BF16)
        for s in range(tm // TK):
            vt_ref[0, s] = vt[:, s * TK:(s + 1) * TK]


def _attn_in(x2d, g, wqkv_bf, wvt_bf, wqi_bf, wkw_bf, tm, seq_len, prompt):
    m = x2d.shape[0]
    row = lambda i: (i, 0)
    row3 = lambda i: (i, 0, 0)
    nqi = N_IDX_HEADS * IDX_DIM
    in_specs = [pl.BlockSpec((tm, D_MODEL), row),
                _const_spec((1, D_MODEL)),
                _const_spec((D_MODEL, 3 * D_MODEL))]
    args = [x2d, g, wqkv_bf]
    if prompt:
        in_specs.append(_const_spec((D_MODEL, D_MODEL)))
        args.append(wvt_bf)
    in_specs += [_const_spec((D_MODEL, nqi)), _const_spec((D_MODEL, LANES))]
    args += [wqi_bf, wkw_bf]
    out_specs = [pl.BlockSpec((tm, D_MODEL), row),
                 pl.BlockSpec((tm, N_HEADS, HEAD_DIM), row3),
                 pl.BlockSpec((tm, N_HEADS, HEAD_DIM), row3),
                 pl.BlockSpec((tm, nqi), row),
                 pl.BlockSpec((tm, LANES), row)]
    out_shape = [jax.ShapeDtypeStruct((m, D_MODEL), BF16),
                 jax.ShapeDtypeStruct((m, N_HEADS, HEAD_DIM), F32),
                 jax.ShapeDtypeStruct((m, N_HEADS, HEAD_DIM), F32),
                 jax.ShapeDtypeStruct((m, nqi), BF16),
                 jax.ShapeDtypeStruct((m, LANES), F32)]
    if prompt:
        steps_per_seq = seq_len // tm
        out_specs += [pl.BlockSpec((tm, D_MODEL), row),
                      pl.BlockSpec((1, tm // TK, D_MODEL, TK),
                                   lambda i: (i // steps_per_seq, i % steps_per_seq, 0, 0))]
        out_shape += [jax.ShapeDtypeStruct((m, D_MODEL), BF16),
                      jax.ShapeDtypeStruct((m // seq_len, seq_len // TK, D_MODEL, TK), BF16)]
    return pl.pallas_call(
        functools.partial(_attn_in_kernel, prompt=prompt, tm=tm),
        grid=(m // tm,),
        in_specs=in_specs,
        out_specs=out_specs,
        out_shape=out_shape,
        compiler_params=_params(("parallel",)),
        name="attn_in",
    )(*args)


def _t5_bucket(d):
    n = jnp.maximum(d, 0)
    max_exact = NUM_BUCKETS // 2
    nf = jnp.maximum(n, 1).astype(F32)
    large = max_exact + (jnp.log(nf / max_exact) / math.log(MAX_DISTANCE / max_exact)
                         * (NUM_BUCKETS - max_exact)).astype(I32)
    large = jnp.minimum(large, NUM_BUCKETS - 1)
    return jnp.where(n < max_exact, n, large)


def _bias_of_bucket(bucket, relb_ref, h):
    val = jnp.full(bucket.shape, relb_ref[NUM_BUCKETS - 1, h], F32)
    for bk in range(NUM_BUCKETS - 2, -1, -1):
        val = jnp.where(bucket == bk, relb_ref[bk, h], val)
    return val


def _prompt_bias_kernel(relb_ref, o_ref):
    h = pl.program_id(0)
    c = lax.broadcasted_iota(I32, (NEAR_TILES * TK, TQ), 0)
    r = lax.broadcasted_iota(I32, (NEAR_TILES * TK, TQ), 1)
    o_ref[0] = _bias_of_bucket(_t5_bucket(r + TK - c), relb_ref, h) * LOG2E


def _prompt_bias(rel_bias):
    return pl.pallas_call(
        _prompt_bias_kernel,
        grid=(N_HEADS,),
        in_specs=[pl.BlockSpec(memory_space=pltpu.SMEM)],
        out_specs=pl.BlockSpec((1, NEAR_TILES * TK, TQ), lambda h: (h, 0, 0)),
        out_shape=jax.ShapeDtypeStruct((N_HEADS, NEAR_TILES * TK, TQ), F32),
        compiler_params=_params(("parallel",)),
        name="prompt_bias",
    )(rel_bias)


def _sample_bias_kernel(relb_ref, o_ref, *, t, n_pages, past_len):
    q = lax.broadcasted_iota(I32, (t, PAGE_ROWS), 0)
    lane = lax.broadcasted_iota(I32, (t, PAGE_ROWS), 1)
    head = lane & (N_HEADS - 1)

    def per_head(value_of_head):
        val = value_of_head(0)
        for h in range(1, N_HEADS):
            val = jnp.where(head == h, value_of_head(h), val)
        return val

    def page(pg, _):
        far = past_len - (pg * PAGE + PAGE - 1) >= FAR_DISTANCE

        @pl.when(far)
        def _():
            o_ref[pg] = per_head(lambda h: jnp.full((t, PAGE_ROWS), relb_ref[NUM_BUCKETS - 1, h], F32))

        @pl.when(jnp.logical_not(far))
        def _():
            bucket = _t5_bucket(past_len + q - (pg * PAGE + (lane >> 3)))
            o_ref[pg] = per_head(lambda h: _bias_of_bucket(bucket, relb_ref, h))

        return 0

    lax.fori_loop(0, n_pages, page, 0)


def _sample_bias(rel_bias, t, past_len):
    n_pages = past_len // PAGE + 1
    return pl.pallas_call(
        functools.partial(_sample_bias_kernel, t=t, n_pages=n_pages, past_len=past_len),
        in_specs=[pl.BlockSpec(memory_space=pltpu.SMEM)],
        out_specs=pl.BlockSpec(memory_space=pltpu.VMEM),
        out_shape=jax.ShapeDtypeStruct((n_pages, t, PAGE_ROWS), F32),
        compiler_params=_params(None),
        name="sample_bias",
    )(rel_bias)


def _sortable(score):
    bits = pltpu.bitcast(score, I32)
    return jnp.where(bits < 0, bits ^ INT_MAX, bits)


def _bisect(count_fn, n_bits, init, shape):
    def body(it, cur):
        trial = cur ^ lax.shift_left(jnp.int32(1), jnp.asarray(n_bits - 1 - it, I32))
        return jnp.where(count_fn(trial), trial, cur)
    return lax.fori_loop(0, n_bits, body, jnp.full(shape, init, I32))


def _prompt_attn_kernel(relb_ref, q_ref, qi_ref, kwq_ref, kw_ref, k_ref, vt_ref, db_ref, o_ref,
                        keys_s, acc_s, m_s, l_s, *, topk):
    i = pl.program_id(1)
    tiles_per_q = TQ // TK
    nt = (i + 1) * tiles_per_q
    nt_dims = (((1,), (1,)), ((), ()))
    row = (1, TQ)

    qi = qi_ref[0]
    qi_h = [qi[:, h * IDX_DIM:(h + 1) * IDX_DIM] for h in range(N_IDX_HEADS)]
    kwq_t = kwq_ref[0].T
    w_scale = (N_IDX_HEADS ** -0.5) * (IDX_DIM ** -0.5)
    w_h = [kwq_t[IDX_DIM + h:IDX_DIM + h + 1, :] * w_scale for h in range(N_IDX_HEADS)]
    krow = lax.broadcasted_iota(I32, (TK, TQ), 0)
    qpos = i * TQ + lax.broadcasted_iota(I32, (TK, TQ), 1)

    def score_chunk(jj, _):
        r0 = pl.multiple_of(jj * TQ, TQ)
        ki_t = kw_ref[0, pl.ds(r0, TQ), 0:IDX_DIM].astype(BF16)
        score = jnp.zeros((TQ, TQ), F32)
        for h in range(N_IDX_HEADS):
            s = lax.dot_general(ki_t, qi_h[h], nt_dims, preferred_element_type=F32)
            score = score + jnp.maximum(s, 0.0) * w_h[h]
        key = _sortable(score)
        for s in range(tiles_per_q):
            j = jj * tiles_per_q + s
            keys_s[j] = jnp.where(j * TK + krow <= qpos, key[s * TK:(s + 1) * TK, :], INT_MIN)
        return 0

    lax.fori_loop(0, i + 1, score_chunk, 0)

    def count(pred_fn):
        def body(j, c):
            m = jnp.where(pred_fn(keys_s[j], j), 1, 0).astype(I32)
            return c + jnp.sum(m.reshape(TK // 8, 8, TQ), axis=0)
        c = lax.fori_loop(0, nt, body, jnp.zeros((8, TQ), I32))
        return jnp.sum(c, axis=0, keepdims=True)

    thr = _bisect(lambda x: count(lambda kt, j: kt >= x) >= topk, 32, INT_MIN, row)
    n_gt = count(lambda kt, j: kt > thr)
    thr1 = jnp.maximum(thr, INT_MIN + 1)
    n_eq = count(lambda kt, j: kt == thr1)
    need = topk - n_gt

    def tie_bound():
        return _bisect(lambda x: count(lambda kt, j: jnp.where(kt == thr, j * TK + krow, INT_MAX) < x) < need,
                       31, 0, row)

    has_tie = jnp.max(n_eq - need) > 0
    xb = lax.cond(has_tie, tie_bound, lambda: jnp.full(row, INT_MAX, I32))

    q = q_ref[0]
    q_h = [q[:, h * HEAD_DIM:(h + 1) * HEAD_DIM] for h in range(N_HEADS)]
    acc_s[...] = jnp.zeros_like(acc_s)
    m_s[...] = jnp.full_like(m_s, NEG_BIG)
    l_s[...] = jnp.zeros_like(l_s)

    def attend(j, near):
        r0 = pl.multiple_of(j * TK, TK)
        kt = keys_s[j]
        tied = jnp.where(j * TK + krow <= xb, 0.0, -jnp.inf)
        amask = jnp.where(kt > thr1, 0.0, jnp.where(kt == thr1, tied, -jnp.inf))
        for h in range(N_HEADS):
            hs = slice(h * HEAD_DIM, (h + 1) * HEAD_DIM)
            k_t = k_ref[0, pl.ds(r0, TK), hs]
            s = lax.dot_general(k_t, q_h[h], nt_dims, preferred_element_type=F32)
            m_old = m_s[h:h + 1, :]
            if near is None:
                c = relb_ref[NUM_BUCKETS - 1, h] * LOG2E
                lg = s + amask
                m_new = jnp.maximum(m_old, jnp.max(lg, axis=0, keepdims=True) + c)
                p = jnp.exp2(lg - (m_new - c))
            else:
                lg = s + (amask + db_ref[h, near * TK:(near + 1) * TK, :])
                m_new = jnp.maximum(m_old, jnp.max(lg, axis=0, keepdims=True))
                p = jnp.exp2(lg - m_new)
            alpha = jnp.exp2(m_old - m_new)
            l_s[h:h + 1, :] = alpha * l_s[h:h + 1, :] + jnp.sum(p, axis=0, keepdims=True)
            m_s[h:h + 1, :] = m_new
            pv = jnp.dot(vt_ref[0, j, hs, :], p.astype(BF16), preferred_element_type=F32)
            acc_s[hs, :] = alpha * acc_s[hs, :] + pv

    first_near = i * tiles_per_q - 1

    def far_tile(j, _):
        attend(j, None)
        return 0

    lax.fori_loop(0, first_near, far_tile, 0)

    @pl.when(i >= 1)
    def _():
        attend(first_near, 0)

    for near in range(1, NEAR_TILES):
        attend(first_near + near, near)

    for h in range(N_HEADS):
        hs = slice(h * HEAD_DIM, (h + 1) * HEAD_DIM)
        o_ref[0, :, hs] = (acc_s[hs, :] / l_s[h:h + 1, :]).T.astype(BF16)


def _prompt_attn(rel_bias, qb, qib, kw, kb, vt, dbias, topk):
    b, t, _ = qb.shape
    nqi = N_IDX_HEADS * IDX_DIM
    assert TK + 1 >= FAR_DISTANCE and TQ % TK == 0
    return pl.pallas_call(
        functools.partial(_prompt_attn_kernel, topk=topk),
        grid=(b, t // TQ),
        in_specs=[pl.BlockSpec(memory_space=pltpu.SMEM),
                  pl.BlockSpec((1, TQ, D_MODEL), lambda bi, i: (bi, i, 0)),
                  pl.BlockSpec((1, TQ, nqi), lambda bi, i: (bi, i, 0)),
                  pl.BlockSpec((1, TQ, LANES), lambda bi, i: (bi, i, 0)),
                  pl.BlockSpec((1, t, LANES), lambda bi, i: (bi, 0, 0)),
                  pl.BlockSpec((1, t, D_MODEL), lambda bi, i: (bi, 0, 0)),
                  pl.BlockSpec((1, t // TK, D_MODEL, TK), lambda bi, i: (bi, 0, 0, 0)),
                  _const_spec((N_HEADS, NEAR_TILES * TK, TQ))],
        out_specs=pl.BlockSpec((1, TQ, D_MODEL), lambda bi, i: (bi, i, 0)),
        out_shape=jax.ShapeDtypeStruct((b, t, D_MODEL), BF16),
        scratch_shapes=[pltpu.VMEM((t // TK, TK, TQ), I32),
                        pltpu.VMEM((D_MODEL, TQ), F32),
                        pltpu.VMEM((N_HEADS, TQ), F32),
                        pltpu.VMEM((N_HEADS, TQ), F32)],
        compiler_params=_params(("parallel", "arbitrary")),
        name="prompt_attn",
    )(rel_bias, qb, qib, kw, kw, kb, vt, dbias)


def _sample_attn_kernel(pt_ref, q_ref, qi_ref, kw_ref, kn_ref, vn_ref, sb_ref, cki_hbm, ck_hbm, cv_hbm,
                        o_ref, ibuf, kbuf, vbuf, keys_s, isem, ksem, vsem,
                        *, t, past_len, topk, page_base):
    b = pl.program_id(0)
    n_idx_chunks = past_len // IDX_CHUNK
    n_pages = past_len // PAGE
    n_kv_chunks = n_pages // KV_PAGES
    n_tiles = n_pages + 1
    rows = N_HEADS * t
    scale = HEAD_DIM ** -0.5
    nt_dims = (((1,), (1,)), ((), ()))

    def idx_copy(c, p):
        page = page_base + pt_ref[b, c * IDX_PAGES + p]
        return pltpu.make_async_copy(cki_hbm.at[page], ibuf.at[c * IDX_PAGES + p], isem.at[c])

    def kv_copies(c, p):
        slot = c % KV_SLOTS
        page = page_base + pt_ref[b, c * KV_PAGES + p]
        dst = pl.ds(p * PAGE_ROWS, PAGE_ROWS)
        return (pltpu.make_async_copy(ck_hbm.at[page], kbuf.at[slot, dst], ksem.at[slot]),
                pltpu.make_async_copy(cv_hbm.at[page], vbuf.at[slot, dst], vsem.at[slot]))

    def start_kv(c):
        for p in range(KV_PAGES):
            for cp in kv_copies(c, p):
                cp.start()

    def wait_kv(c):
        for p in range(KV_PAGES):
            for cp in kv_copies(c, p):
                cp.wait()

    for c in range(n_idx_chunks):
        for p in range(IDX_PAGES):
            idx_copy(c, p).start()
    for c in range(KV_SLOTS - 1):
        start_kv(c)

    qi = qi_ref[0].astype(F32)
    qi_rows = jnp.concatenate([qi[:, h * IDX_DIM:(h + 1) * IDX_DIM] for h in range(N_IDX_HEADS)],
                              axis=0).astype(BF16)
    w = kw_ref[0][:, IDX_DIM:IDX_DIM + N_IDX_HEADS] * (N_IDX_HEADS ** -0.5) * (IDX_DIM ** -0.5)
    w_h = [w[:, h:h + 1] for h in range(N_IDX_HEADS)]

    def scores(ki_bf, transposed):
        if transposed:
            s = jnp.dot(qi_rows, ki_bf, preferred_element_type=F32)
        else:
            s = lax.dot_general(qi_rows, ki_bf, nt_dims, preferred_element_type=F32)
        score = jnp.maximum(s[0:t], 0.0) * w_h[0]
        for h in range(1, N_IDX_HEADS):
            score = score + jnp.maximum(s[h * t:(h + 1) * t], 0.0) * w_h[h]
        return score

    for c in range(n_idx_chunks):
        for p in range(IDX_PAGES):
            idx_copy(c, p).wait()
        ki_t = jnp.concatenate([ibuf[c * IDX_PAGES + p] for p in range(IDX_PAGES)], axis=1)
        key = _sortable(scores(ki_t.astype(BF16), True))
        for k in range(IDX_PAGES):
            keys_s[c * IDX_PAGES + k] = key[:, k * PAGE:(k + 1) * PAGE]

    ki_new = jnp.concatenate([kw_ref[0][:, 0:IDX_DIM], jnp.zeros((PAGE - t, IDX_DIM), F32)], axis=0)
    qrow = lax.broadcasted_iota(I32, (t, PAGE), 0)
    lane = lax.broadcasted_iota(I32, (t, PAGE), 1)
    keys_s[n_tiles - 1] = jnp.where(lane <= qrow, _sortable(scores(ki_new.astype(BF16), False)), INT_MIN)

    pos = (lax.broadcasted_iota(I32, (n_tiles, t, PAGE), 0) * PAGE
           + lax.broadcasted_iota(I32, (n_tiles, t, PAGE), 2))

    def count(pred):
        m = jnp.where(pred, 1, 0).astype(I32)
        part = m[0:n_pages]
        for group in (4, 4):
            part = jnp.sum(part.reshape(group, part.shape[0] // group, t, PAGE), axis=0)
        per_lane = jnp.sum(part, axis=0) + m[n_pages]
        return jnp.sum(per_lane, axis=1, keepdims=True)

    thr = _bisect(lambda x: count(keys_s[...] >= x[None]) >= topk, 32, INT_MIN, (t, 1))
    kall = keys_s[...]
    need = topk - count(kall > thr[None])
    thr1 = jnp.maximum(thr, INT_MIN + 1)[None]

    def tie_bound():
        return _bisect(lambda x: count((keys_s[...] == thr[None]) & (pos < x[None])) < need, 31, 0, (t, 1))

    has_tie = jnp.max(count(kall == thr1) - need) > 0
    xb = lax.cond(has_tie, tie_bound, lambda: jnp.full((t, 1), INT_MAX, I32))
    tied = jnp.where(pos <= xb[None], 0.0, NEG_BIG)
    keys_s[...] = pltpu.bitcast(
        jnp.where(kall > thr1, 0.0, jnp.where(kall == thr1, tied, NEG_BIG)).astype(F32), I32)

    q = q_ref[0].astype(F32)
    q_rows = jnp.concatenate([q[:, h * HEAD_DIM:(h + 1) * HEAD_DIM] for h in range(N_HEADS)],
                             axis=0).astype(BF16)

    def head_of_lane(n):
        return lax.broadcasted_iota(I32, (t, n), 1) & (N_HEADS - 1)

    def expand_mask(tile_bits, n):
        am = pltpu.bitcast(tile_bits, F32)
        key_of_lane = lax.broadcasted_iota(I32, (t, LANES), 1) >> 3
        return jnp.concatenate([jnp.take_along_axis(am, c * (LANES // N_HEADS) + key_of_lane, axis=1)
                                for c in range(n // LANES)], axis=1)

    def page_logits(x_bf, bias, tile_bits):
        n = x_bf.shape[0]
        r = lax.dot_general(q_rows, x_bf, nt_dims, preferred_element_type=F32)
        head = head_of_lane(n)
        lg = r[0:t]
        for h in range(1, N_HEADS):
            lg = jnp.where(head == h, r[h * t:(h + 1) * t], lg)
        return lg * scale + bias + expand_mask(tile_bits, n)

    def per_head_allreduce(x, op):
        for shift in (8, 16, 32, 64):
            x = op(x, pltpu.roll(x, shift, axis=1))
        return x

    def weighted(p, x_bf):
        head = head_of_lane(p.shape[1])
        p_rows = jnp.concatenate([jnp.where(head == h, p, 0.0) for h in range(N_HEADS)],
                                 axis=0).astype(BF16)
        return jnp.dot(p_rows, x_bf, preferred_element_type=F32)

    head128 = head_of_lane(LANES)

    def absorb(carry, lg_list, xv_list):
        m_run, l_run, acc = carry
        m_tile = m_run
        for lg in lg_list:
            for s in range(lg.shape[1] // LANES):
                m_tile = jnp.maximum(m_tile, lg[:, s * LANES:(s + 1) * LANES])
        m_new = per_head_allreduce(m_tile, jnp.maximum)
        alpha = jnp.exp(m_run - m_new)
        l_run = alpha * l_run
        pv = jnp.zeros((rows, HEAD_DIM), F32)
        for lg, xv in zip(lg_list, xv_list):
            groups = lg.shape[1] // LANES
            pr = jnp.exp(lg - jnp.concatenate([m_new] * groups, axis=1))
            for s in range(groups):
                l_run = l_run + pr[:, s * LANES:(s + 1) * LANES]
            pv = pv + weighted(pr, xv)
        alpha_rows = jnp.concatenate(
            [jnp.max(jnp.where(head128 == h, alpha, 0.0), axis=1, keepdims=True) for h in range(N_HEADS)],
            axis=0)
        return m_new, l_run, alpha_rows * acc + pv

    def kv_chunk(c, carry):
        slot = c % KV_SLOTS

        @pl.when(c + KV_SLOTS - 1 < n_kv_chunks)
        def _():
            start_kv(c + KV_SLOTS - 1)

        wait_kv(c)
        lgs, xvs = [], []
        for p in range(KV_PAGES):
            pg = c * KV_PAGES + p
            rows_p = slice(p * PAGE_ROWS, (p + 1) * PAGE_ROWS)
            lgs.append(page_logits(kbuf[slot, rows_p, :].astype(BF16), sb_ref[pg], keys_s[pg]))
            xvs.append(vbuf[slot, rows_p, :].astype(BF16))
        return absorb(carry, lgs, xvs)

    carry = (jnp.full((t, LANES), 0.1 * NEG_BIG, F32), jnp.zeros((t, LANES), F32),
             jnp.zeros((rows, HEAD_DIM), F32))
    carry = lax.fori_loop(0, n_kv_chunks, kv_chunk, carry)

    new_rows = t * N_HEADS
    pad = jnp.zeros((LANES - new_rows, HEAD_DIM), F32)
    k_new = jnp.concatenate([kn_ref[0], pad], axis=0).astype(BF16)
    v_new = jnp.concatenate([vn_ref[0], pad], axis=0).astype(BF16)
    lg_new = page_logits(k_new, sb_ref[n_pages][:, 0:LANES], keys_s[n_tiles - 1])
    lg_new = jnp.where(lax.broadcasted_iota(I32, (t, LANES), 1) < new_rows, lg_new, NEG_BIG)
    _, l_run, acc = absorb(carry, [lg_new], [v_new])
    l_fin = per_head_allreduce(l_run, jnp.add)
    for h in range(N_HEADS):
        o_ref[0, :, h * HEAD_DIM:(h + 1) * HEAD_DIM] = (acc[h * t:(h + 1) * t, :] / l_fin[:, h:h + 1]).astype(BF16)


def _sample_attn(page_table, qb, qib, kw, k_new, v_new, sbias, cki, ck, cv, topk, page_base):
    b, t, _ = qb.shape
    n_pages = page_table.shape[1]
    past_len = n_pages * PAGE
    nqi = N_IDX_HEADS * IDX_DIM
    assert t * N_HEADS <= LANES and n_pages % KV_PAGES == 0 and n_pages % IDX_PAGES == 0
    assert n_pages // KV_PAGES >= KV_SLOTS - 1
    per_b = lambda bi, pt: (bi, 0, 0)
    grid_spec = pltpu.PrefetchScalarGridSpec(
        num_scalar_prefetch=1,
        grid=(b,),
        in_specs=[pl.BlockSpec((1, t, D_MODEL), per_b),
                  pl.BlockSpec((1, t, nqi), per_b),
                  pl.BlockSpec((1, t, LANES), per_b),
                  pl.BlockSpec((1, t * N_HEADS, HEAD_DIM), per_b),
                  pl.BlockSpec((1, t * N_HEADS, HEAD_DIM), per_b),
                  pl.BlockSpec(sbias.shape, lambda bi, pt: (0, 0, 0), pipeline_mode=pl.Buffered(1)),
                  pl.BlockSpec(memory_space=pl.ANY),
                  pl.BlockSpec(memory_space=pl.ANY),
                  pl.BlockSpec(memory_space=pl.ANY)],
        out_specs=pl.BlockSpec((1, t, D_MODEL), per_b),
        scratch_shapes=[pltpu.VMEM((n_pages, IDX_DIM, PAGE), F32),
                        pltpu.VMEM((KV_SLOTS, KV_PAGES * PAGE_ROWS, HEAD_DIM), F32),
                        pltpu.VMEM((KV_SLOTS, KV_PAGES * PAGE_ROWS, HEAD_DIM), F32),
                        pltpu.VMEM((n_pages + 1, t, PAGE), I32),
                        pltpu.SemaphoreType.DMA((n_pages // IDX_PAGES,)),
                        pltpu.SemaphoreType.DMA((KV_SLOTS,)),
                        pltpu.SemaphoreType.DMA((KV_SLOTS,))])
    return pl.pallas_call(
        functools.partial(_sample_attn_kernel, t=t, past_len=past_len, topk=topk, page_base=page_base),
        grid_spec=grid_spec,
        out_shape=jax.ShapeDtypeStruct((b, t, D_MODEL), BF16),
        compiler_params=_params(("arbitrary",)),
        name="sample_attn",
    )(page_table, qb, qib, kw, k_new, v_new, sbias, cki, ck, cv)


def kernel(x_prompt, x_sample, state_conv_a, state_conv_b, cache_k, cache_v, cache_kidx, page_table,
           norm_conv, w_in_conv, conv_a_w, conv_b_w, conv_b_bias, ln_b_gain, ln_b_bias, w_out_conv,
           norm_attn, w_in_attn, w_out_attn, rel_bias, norm_mlp, w_up, w_down, norm_final):
    depth = norm_mlp.shape[0]
    n_pool = cache_k.shape[1]
    past_len = page_table.shape[1] * PAGE
    nqkv = 3 * D_MODEL
    nqi = N_IDX_HEADS * IDX_DIM
    row = lambda a: a.reshape(1, -1)

    cki_flat = jnp.swapaxes(cache_kidx, 2, 3).reshape(-1, IDX_DIM, PAGE)
    ck_flat = cache_k.reshape(-1, PAGE_ROWS, HEAD_DIM)
    cv_flat = cache_v.reshape(-1, PAGE_ROWS, HEAD_DIM)
    dbias = _prompt_bias(rel_bias)
    sbias = _sample_bias(rel_bias, x_sample.shape[1], past_len)

    def run(x, sample):
        b, t, _ = x.shape
        m = b * t
        tm = min(512, m)
        x2d = x.reshape(m, D_MODEL)
        new_a, new_b, new_k, new_v, new_ki = [], [], [], [], []
        for layer in range(depth):
            i = layer // 2
            last = layer == depth - 1
            if layer % 2 == 0:
                if sample:
                    buf_a, buf_b = state_conv_a[i], state_conv_b[i]
                else:
                    buf_a = jnp.zeros((b, CONV_A_WIDTH - 1, D_A), F32)
                    buf_b = jnp.zeros((b, CONV_B_WIDTH - 1, D_B), F32)
                sa = jnp.pad(buf_a, ((0, 0), (A_HALO - (CONV_A_WIDTH - 1), 0), (0, 0)))
                sb = jnp.pad(buf_b, ((0, 0), (B_HALO - (CONV_B_WIDTH - 1), 0), (0, 0)))
                gb, cx, u = _conv_in(x2d, row(norm_conv[i]), w_in_conv[i].astype(BF16), tm)
                tt = min(512, t)
                y, na, nb = _conv(gb.reshape(b, t, D_A), cx.reshape(b, t, D_A), u.reshape(b, t, D_B), sa, sb,
                                  conv_a_w[i], conv_b_w[i], row(conv_b_bias[i]), row(ln_b_gain[i]),
                                  row(ln_b_bias[i]), tt, min(64, tt))
                new_a.append(na)
                new_b.append(nb)
                mix = y.reshape(m, D_MODEL)
                w_out = w_out_conv[i]
            else:
                w_in = w_in_attn[i]
                wkw = jnp.pad(w_in[:, nqkv + nqi:], ((0, 0), (0, LANES - IDX_DIM - N_IDX_HEADS)))
                wvt = w_in[:, 2 * D_MODEL:nqkv].T.astype(BF16)
                outs = _attn_in(x2d, row(norm_attn[i]), w_in[:, :nqkv].astype(BF16), wvt,
                                w_in[:, nqkv:nqkv + nqi].astype(BF16), wkw.astype(BF16), tm, t, not sample)
                qb, k, v, qib, kw = outs[:5]
                shp = lambda a: a.reshape(b, t, a.shape[-1])
                if sample:
                    topk = min(TOPK_MAX, (past_len + t) // 4)
                    o = _sample_attn(page_table, shp(qb), shp(qib), shp(kw),
                                     k.reshape(b, t * N_HEADS, HEAD_DIM), v.reshape(b, t * N_HEADS, HEAD_DIM),
                                     sbias, cki_flat, ck_flat, cv_flat, topk, i * n_pool)
                else:
                    topk = min(TOPK_MAX, t // 4)
                    kb, vt = outs[5:]
                    o = _prompt_attn(rel_bias, shp(qb), shp(qib), shp(kw), shp(kb), vt, dbias, topk)
                new_k.append(k.reshape(b, t, N_HEADS, HEAD_DIM))
                new_v.append(v.reshape(b, t, N_HEADS, HEAD_DIM))
                new_ki.append(kw[:, :IDX_DIM].reshape(b, t, IDX_DIM))
                mix = o.reshape(m, D_MODEL)
                w_out = w_out_attn[i]
            x2d = _out_mlp(mix, x2d, w_out.astype(BF16), row(norm_mlp[layer]), w_up[layer].astype(BF16),
                           w_down[layer].astype(BF16), row(norm_final), tm, last)
        return (x2d.reshape(b, t, D_MODEL), jnp.stack(new_a), jnp.stack(new_b),
                jnp.stack(new_k), jnp.stack(new_v), jnp.stack(new_ki))

    y_p, ca_p, cb_p, k_p, v_p, ki_p = run(x_prompt, False)
    y_s, ca_s, cb_s, k_s, v_s, ki_s = run(x_sample, True)
    return (y_p, y_s, ca_p, cb_p, k_p, v_p, ki_p, ca_s, cb_s, k_s, v_s, ki_s)
```

```python
import functools
import math

import jax
import jax.numpy as jnp
from jax import lax
from jax.experimental import pallas as pl
from jax.experimental.pallas import tpu as pltpu

F32 = jnp.float32
BF16 = jnp.bfloat16
I32 = jnp.int32

D_MODEL = 1024
D_A = 512
D_B = 512
CONV_A_WIDTH = 3
CONV_B_WIDTH = 31
N_HEADS = 8
HEAD_DIM = 128
N_IDX_HEADS = 8
IDX_DIM = 64
TOPK_MAX = 256
NUM_BUCKETS = 32
MAX_DISTANCE = 128
D_FF = 4096
EPS = 1e-6
PAGE = 128
LANES = 128
SUBLANES = 8

INT_MIN = -(2 ** 31)
INT_MAX = 2 ** 31 - 1
NEG_BIG = -1e30

VMEM_LIMIT_BYTES = 56 * 1024 * 1024

A_HALO = 8
B_HALO = 32

TQ = 256
TK = 128
NEAR_TILES = TQ // TK + 1
def _first_far_distance():
    exact = NUM_BUCKETS // 2
    d = exact
    while exact + int(math.log(d / exact) / math.log(MAX_DISTANCE / exact) * (NUM_BUCKETS - exact)) < NUM_BUCKETS - 1:
        d += 1
    return d


FAR_DISTANCE = _first_far_distance()
LOG2E = math.log2(math.e)
QK_SCALE = (HEAD_DIM ** -0.5) * LOG2E
IDX_PAGES = 16
IDX_CHUNK = IDX_PAGES * PAGE
PAGE_ROWS = PAGE * N_HEADS


def _params(sem):
    return pltpu.CompilerParams(dimension_semantics=sem, vmem_limit_bytes=VMEM_LIMIT_BYTES)


def _const_spec(shape):
    nd = len(shape)
    return pl.BlockSpec(shape, lambda *_: (0,) * nd, pipeline_mode=pl.Buffered(1))


def _rms_bf16(x, g):
    ms = jnp.mean(x * x, axis=-1, keepdims=True)
    return ((x * lax.rsqrt(ms + EPS)) * g).astype(BF16)


def _sigmoid(x):
    return 1.0 / (1.0 + jnp.exp(-x))


def _conv_in_kernel(x_ref, g_ref, w_ref, gb_ref, cx_ref, u_ref):
    xn = _rms_bf16(x_ref[...], g_ref[...])
    proj = jnp.dot(xn, w_ref[...], preferred_element_type=F32)
    gb_ref[...] = proj[:, :D_A]
    cx_ref[...] = proj[:, D_A:2 * D_A] * proj[:, 2 * D_A:3 * D_A]
    u_ref[...] = proj[:, 3 * D_A:3 * D_A + D_B] * _sigmoid(proj[:, 3 * D_A + D_B:])


def _conv_in(x2d, g, w_bf, tm):
    m = x2d.shape[0]
    n = w_bf.shape[1]
    out = jax.ShapeDtypeStruct((m, D_A), F32)
    return pl.pallas_call(
        _conv_in_kernel,
        grid=(m // tm,),
        in_specs=[pl.BlockSpec((tm, D_MODEL), lambda i: (i, 0)),
                  _const_spec((1, D_MODEL)),
                  _const_spec((D_MODEL, n))],
        out_specs=[pl.BlockSpec((tm, D_A), lambda i: (i, 0))] * 3,
        out_shape=[out, out, out],
        compiler_params=_params(("parallel",)),
        name="conv_in",
    )(x2d, g, w_bf)


def _conv_kernel(gb_ref, cx_ref, u_ref, sa_ref, sb_ref, wa_ref, wb_ref, bias_ref, lng_ref, lnb_ref,
                 y_ref, na_ref, nb_ref, xa_s, xb_s, sh_s, *, tt, rc):
    t = pl.program_id(1)

    @pl.when(t == 0)
    def _():
        xa_s[0:A_HALO, :] = sa_ref[0]
        xb_s[0:B_HALO, :] = sb_ref[0]

    @pl.when(t > 0)
    def _():
        xa_s[0:A_HALO, :] = xa_s[tt:tt + A_HALO, :]
        xb_s[0:B_HALO, :] = xb_s[tt:tt + B_HALO, :]

    xa_s[A_HALO:A_HALO + tt, :] = cx_ref[0]
    xb_s[B_HALO:B_HALO + tt, :] = u_ref[0]

    bias = bias_ref[...]
    lng = lng_ref[...]
    lnb = lnb_ref[...]
    a0 = A_HALO - (CONV_A_WIDTH - 1)
    b0 = B_HALO - (CONV_B_WIDTH - 1)
    for c in range(tt // rc):
        r0 = c * rc
        ya = wa_ref[0:1, :] * xa_s[a0 + r0:a0 + r0 + rc, :]
        for j in range(1, CONV_A_WIDTH):
            ya = ya + wa_ref[j:j + 1, :] * xa_s[a0 + j + r0:a0 + j + r0 + rc, :]
        y_ref[0, r0:r0 + rc, 0:D_A] = gb_ref[0, r0:r0 + rc, :] * ya

        yb = None
        for shift in range(SUBLANES):
            offs = [o for o in range(b0, b0 + CONV_B_WIDTH) if o % SUBLANES == shift]
            if not offs:
                continue
            rows_needed = offs[-1] - shift + rc
            sh_s[shift, 0:rows_needed, :] = xb_s[r0 + shift:r0 + shift + rows_needed, :]
            for o in offs:
                term = wb_ref[o - b0:o - b0 + 1, :] * sh_s[shift, o - shift:o - shift + rc, :]
                yb = term if yb is None else yb + term
        yb = yb + bias
        mu = jnp.mean(yb, axis=-1, keepdims=True)
        var = jnp.mean(jnp.square(yb - mu), axis=-1, keepdims=True)
        z = (yb - mu) * lax.rsqrt(var + EPS) * lng + lnb
        y_ref[0, r0:r0 + rc, D_A:D_A + D_B] = z * _sigmoid(z)

    na_ref[0] = xa_s[tt + A_HALO - (CONV_A_WIDTH - 1):tt + A_HALO, :]
    nb_ref[0] = xb_s[tt + B_HALO - (CONV_B_WIDTH - 1):tt + B_HALO, :]


def _conv(gb, cx, u, sa, sb, wa, wb, bias, lng, lnb, tt, rc):
    b, t, _ = cx.shape
    row = lambda i, j: (i, j, 0)
    per_b = lambda i, j: (i, 0, 0)
    return pl.pallas_call(
        functools.partial(_conv_kernel, tt=tt, rc=rc),
        grid=(b, t // tt),
        in_specs=[pl.BlockSpec((1, tt, D_A), row),
                  pl.BlockSpec((1, tt, D_A), row),
                  pl.BlockSpec((1, tt, D_B), row),
                  pl.BlockSpec((1, A_HALO, D_A), per_b),
                  pl.BlockSpec((1, B_HALO, D_B), per_b),
                  pl.BlockSpec((CONV_A_WIDTH, D_A), lambda i, j: (0, 0)),
                  pl.BlockSpec((CONV_B_WIDTH, D_B), lambda i, j: (0, 0)),
                  pl.BlockSpec((1, D_B), lambda i, j: (0, 0)),
                  pl.BlockSpec((1, D_B), lambda i, j: (0, 0)),
                  pl.BlockSpec((1, D_B), lambda i, j: (0, 0))],
        out_specs=[pl.BlockSpec((1, tt, D_MODEL), row),
                   pl.BlockSpec((1, CONV_A_WIDTH - 1, D_A), per_b),
                   pl.BlockSpec((1, CONV_B_WIDTH - 1, D_B), per_b)],
        out_shape=[jax.ShapeDtypeStruct((b, t, D_MODEL), F32),
                   jax.ShapeDtypeStruct((b, CONV_A_WIDTH - 1, D_A), F32),
                   jax.ShapeDtypeStruct((b, CONV_B_WIDTH - 1, D_B), F32)],
        scratch_shapes=[pltpu.VMEM((A_HALO + tt, D_A), F32),
                        pltpu.VMEM((B_HALO + tt, D_B), F32),
                        pltpu.VMEM((SUBLANES, B_HALO + rc, D_B), F32)],
        compiler_params=_params(("parallel", "arbitrary")),
        name="conv_mix",
    )(gb, cx, u, sa, sb, wa, wb, bias, lng, lnb)


def _out_mlp_kernel(a_ref, res_ref, wo_ref, g_ref, wu_ref, wd_ref, gf_ref, o_ref, *, final_norm, ff_chunk):
    x1 = res_ref[...] + jnp.dot(a_ref[...].astype(BF16), wo_ref[...], preferred_element_type=F32)
    xn = _rms_bf16(x1, g_ref[...])
    acc = x1
    for c in range(D_FF // ff_chunk):
        h = jnp.dot(xn, wu_ref[:, c * ff_chunk:(c + 1) * ff_chunk], preferred_element_type=F32)
        h = jnp.square(jnp.maximum(h, 0.0)).astype(BF16)
        acc = acc + jnp.dot(h, wd_ref[c * ff_chunk:(c + 1) * ff_chunk, :], preferred_element_type=F32)
    if final_norm:
        ms = jnp.mean(acc * acc, axis=-1, keepdims=True)
        acc = (acc * lax.rsqrt(ms + EPS)) * gf_ref[...]
    o_ref[...] = acc


def _out_mlp(a2d, res2d, wo_bf, g, wu_bf, wd_bf, gf, tm, final_norm):
    m = res2d.shape[0]
    row = lambda i: (i, 0)
    return pl.pallas_call(
        functools.partial(_out_mlp_kernel, final_norm=final_norm, ff_chunk=1024),
        grid=(m // tm,),
        in_specs=[pl.BlockSpec((tm, D_MODEL), row),
                  pl.BlockSpec((tm, D_MODEL), row),
                  _const_spec((D_MODEL, D_MODEL)),
                  _const_spec((1, D_MODEL)),
                  _const_spec((D_MODEL, D_FF)),
                  _const_spec((D_FF, D_MODEL)),
                  _const_spec((1, D_MODEL))],
        out_specs=pl.BlockSpec((tm, D_MODEL), row),
        out_shape=jax.ShapeDtypeStruct((m, D_MODEL), F32),
        compiler_params=_params(("parallel",)),
        name="out_mlp",
    )(a2d, res2d, wo_bf, g, wu_bf, wd_bf, gf)


def _attn_in_kernel(*refs, prompt, tm):
    if prompt:
        (x_ref, g_ref, wqkv_ref, wvt_ref, wqi_ref, wkw_ref,
         qb_ref, k_ref, v_ref, qib_ref, kw_ref, kb_ref, vt_ref) = refs
    else:
        x_ref, g_ref, wqkv_ref, wqi_ref, wkw_ref, qb_ref, k_ref, v_ref, qib_ref, kw_ref = refs
    xn = _rms_bf16(x_ref[...], g_ref[...])
    q = jnp.dot(xn, wqkv_ref[:, 0:D_MODEL], preferred_element_type=F32)
    qb_ref[...] = (q * QK_SCALE if prompt else q).astype(BF16)
    k = jnp.dot(xn, wqkv_ref[:, D_MODEL:2 * D_MODEL], preferred_element_type=F32)
    v = jnp.dot(xn, wqkv_ref[:, 2 * D_MODEL:3 * D_MODEL], preferred_element_type=F32)
    for h in range(N_HEADS):
        k_ref[:, h, :] = k[:, h * HEAD_DIM:(h + 1) * HEAD_DIM]
        v_ref[:, h, :] = v[:, h * HEAD_DIM:(h + 1) * HEAD_DIM]
    qib_ref[...] = jnp.dot(xn, wqi_ref[...], preferred_element_type=F32).astype(BF16)
    kw_ref[...] = jnp.dot(xn, wkw_ref[...], preferred_element_type=F32)
    if prompt:
        kb_ref[...] = k.astype(BF16)
        vt = lax.dot_general(wvt_ref[...], xn, (((1,), (1,)), ((), ())),
                             preferred_element_type=F32).astype(BF16)
        for s in range(tm // TK):
            vt_ref[0, s] = vt[:, s * TK:(s + 1) * TK]


def _attn_in(x2d, g, wqkv_bf, wvt_bf, wqi_bf, wkw_bf, tm, seq_len, prompt):
    m = x2d.shape[0]
    row = lambda i: (i, 0)
    row3 = lambda i: (i, 0, 0)
    nqi = N_IDX_HEADS * IDX_DIM
    in_specs = [pl.BlockSpec((tm, D_MODEL), row),
                _const_spec((1, D_MODEL)),
                _const_spec((D_MODEL, 3 * D_MODEL))]
    args = [x2d, g, wqkv_bf]
    if prompt:
        in_specs.append(_const_spec((D_MODEL, D_MODEL)))
        args.append(wvt_bf)
    in_specs += [_const_spec((D_MODEL, nqi)), _const_spec((D_MODEL, LANES))]
    args += [wqi_bf, wkw_bf]
    out_specs = [pl.BlockSpec((tm, D_MODEL), row),
                 pl.BlockSpec((tm, N_HEADS, HEAD_DIM), row3),
                 pl.BlockSpec((tm, N_HEADS, HEAD_DIM), row3),
                 pl.BlockSpec((tm, nqi), row),
                 pl.BlockSpec((tm, LANES), row)]
    out_shape = [jax.ShapeDtypeStruct((m, D_MODEL), BF16),
                 jax.ShapeDtypeStruct((m, N_HEADS, HEAD_DIM), F32),
                 jax.ShapeDtypeStruct((m, N_HEADS, HEAD_DIM), F32),
                 jax.ShapeDtypeStruct((m, nqi), BF16),
                 jax.ShapeDtypeStruct((m, LANES), F32)]
    if prompt:
        steps_per_seq = seq_len // tm
        out_specs += [pl.BlockSpec((tm, D_MODEL), row),
                      pl.BlockSpec((1, tm // TK, D_MODEL, TK),
                                   lambda i: (i // steps_per_seq, i % steps_per_seq, 0, 0))]
        out_shape += [jax.ShapeDtypeStruct((m, D_MODEL), BF16),
                      jax.ShapeDtypeStruct((m // seq_len, seq_len // TK, D_MODEL, TK), BF16)]
    return pl.pallas_call(
        functools.partial(_attn_in_kernel, prompt=prompt, tm=tm),
        grid=(m // tm,),
        in_specs=in_specs,
        out_specs=out_specs,
        out_shape=out_shape,
        compiler_params=_params(("parallel",)),
        name="attn_in",
    )(*args)


def _t5_bucket(d):
    n = jnp.maximum(d, 0)
    max_exact = NUM_BUCKETS // 2
    nf = jnp.maximum(n, 1).astype(F32)
    large = max_exact + (jnp.log(nf / max_exact) / math.log(MAX_DISTANCE / max_exact)
                         * (NUM_BUCKETS - max_exact)).astype(I32)
    large = jnp.minimum(large, NUM_BUCKETS - 1)
    return jnp.where(n < max_exact, n, large)


def _bias_of_bucket(bucket, relb_ref, h):
    val = jnp.full(bucket.shape, relb_ref[NUM_BUCKETS - 1, h], F32)
    for bk in range(NUM_BUCKETS - 2, -1, -1):
        val = jnp.where(bucket == bk, relb_ref[bk, h], val)
    return val


def _prompt_bias_kernel(relb_ref, o_ref):
    h = pl.program_id(0)
    c = lax.broadcasted_iota(I32, (NEAR_TILES * TK, TQ), 0)
    r = lax.broadcasted_iota(I32, (NEAR_TILES * TK, TQ), 1)
    o_ref[0] = _bias_of_bucket(_t5_bucket(r + TK - c), relb_ref, h) * LOG2E


def _prompt_bias(rel_bias):
    return pl.pallas_call(
        _prompt_bias_kernel,
        grid=(N_HEADS,),
        in_specs=[pl.BlockSpec(memory_space=pltpu.SMEM)],
        out_specs=pl.BlockSpec((1, NEAR_TILES * TK, TQ), lambda h: (h, 0, 0)),
        out_shape=jax.ShapeDtypeStruct((N_HEADS, NEAR_TILES * TK, TQ), F32),
        compiler_params=_params(("parallel",)),
        name="prompt_bias",
    )(rel_bias)


def _sortable(score):
    bits = pltpu.bitcast(score, I32)
    return jnp.where(bits < 0, bits ^ INT_MAX, bits)


def _bisect(count_fn, n_bits, init, shape):
    def body(it, cur):
        trial = cur ^ lax.shift_left(jnp.int32(1), jnp.asarray(n_bits - 1 - it, I32))
        return jnp.where(count_fn(trial), trial, cur)
    return lax.fori_loop(0, n_bits, body, jnp.full(shape, init, I32))


def _prompt_attn_kernel(relb_ref, q_ref, qi_ref, kwq_ref, kw_ref, k_ref, vt_ref, db_ref, o_ref,
                        keys_s, acc_s, m_s, l_s, *, topk):
    i = pl.program_id(1)
    tiles_per_q = TQ // TK
    nt = (i + 1) * tiles_per_q
    nt_dims = (((1,), (1,)), ((), ()))
    row = (1, TQ)

    qi = qi_ref[0]
    qi_h = [qi[:, h * IDX_DIM:(h + 1) * IDX_DIM] for h in range(N_IDX_HEADS)]
    kwq_t = kwq_ref[0].T
    w_scale = (N_IDX_HEADS ** -0.5) * (IDX_DIM ** -0.5)
    w_h = [kwq_t[IDX_DIM + h:IDX_DIM + h + 1, :] * w_scale for h in range(N_IDX_HEADS)]
    krow = lax.broadcasted_iota(I32, (TK, TQ), 0)
    qpos = i * TQ + lax.broadcasted_iota(I32, (TK, TQ), 1)

    def score_chunk(jj, _):
        r0 = pl.multiple_of(jj * TQ, TQ)
        ki_t = kw_ref[0, pl.ds(r0, TQ), 0:IDX_DIM].astype(BF16)
        score = jnp.zeros((TQ, TQ), F32)
        for h in range(N_IDX_HEADS):
            s = lax.dot_general(ki_t, qi_h[h], nt_dims, preferred_element_type=F32)
            score = score + jnp.maximum(s, 0.0) * w_h[h]
        key = _sortable(score)
        for s in range(tiles_per_q):
            j = jj * tiles_per_q + s
            keys_s[j] = jnp.where(j * TK + krow <= qpos, key[s * TK:(s + 1) * TK, :], INT_MIN)
        return 0

    lax.fori_loop(0, i + 1, score_chunk, 0)

    def count(pred_fn):
        def body(j, c):
            m = jnp.where(pred_fn(keys_s[j], j), 1, 0).astype(I32)
            return c + jnp.sum(m.reshape(TK // 8, 8, TQ), axis=0)
        c = lax.fori_loop(0, nt, body, jnp.zeros((8, TQ), I32))
        return jnp.sum(c, axis=0, keepdims=True)

    thr = _bisect(lambda x: count(lambda kt, j: kt >= x) >= topk, 32, INT_MIN, row)
    n_gt = count(lambda kt, j: kt > thr)
    thr1 = jnp.maximum(thr, INT_MIN + 1)
    n_eq = count(lambda kt, j: kt == thr1)
    need = topk - n_gt

    def tie_bound():
        return _bisect(lambda x: count(lambda kt, j: jnp.where(kt == thr, j * TK + krow, INT_MAX) < x) < need,
                       31, 0, row)

    has_tie = jnp.max(n_eq - need) > 0
    xb = lax.cond(has_tie, tie_bound, lambda: jnp.full(row, INT_MAX, I32))

    q = q_ref[0]
    q_h = [q[:, h * HEAD_DIM:(h + 1) * HEAD_DIM] for h in range(N_HEADS)]
    acc_s[...] = jnp.zeros_like(acc_s)
    m_s[...] = jnp.full_like(m_s, NEG_BIG)
    l_s[...] = jnp.zeros_like(l_s)

    def attend(j, near):
        r0 = pl.multiple_of(j * TK, TK)
        kt = keys_s[j]
        tied = jnp.where(j * TK + krow <= xb, 0.0, -jnp.inf)
        amask = jnp.where(kt > thr1, 0.0, jnp.where(kt == thr1, tied, -jnp.inf))
        for h in range(N_HEADS):
            hs = slice(h * HEAD_DIM, (h + 1) * HEAD_DIM)
            k_t = k_ref[0, pl.ds(r0, TK), hs]
            s = lax.dot_general(k_t, q_h[h], nt_dims, preferred_element_type=F32)
            m_old = m_s[h:h + 1, :]
            if near is None:
                c = relb_ref[NUM_BUCKETS - 1, h] * LOG2E
                lg = s + amask
                m_new = jnp.maximum(m_old, jnp.max(lg, axis=0, keepdims=True) + c)
                p = jnp.exp2(lg - (m_new - c))
            else:
                lg = s + (amask + db_ref[h, near * TK:(near + 1) * TK, :])
                m_new = jnp.maximum(m_old, jnp.max(lg, axis=0, keepdims=True))
                p = jnp.exp2(lg - m_new)
            alpha = jnp.exp2(m_old - m_new)
            l_s[h:h + 1, :] = alpha * l_s[h:h + 1, :] + jnp.sum(p, axis=0, keepdims=True)
            m_s[h:h + 1, :] = m_new
            pv = jnp.dot(vt_ref[0, j, hs, :], p.astype(BF16), preferred_element_type=F32)
            acc_s[hs, :] = alpha * acc_s[hs, :] + pv

    first_near = i * tiles_per_q - 1

    def far_tile(j, _):
        attend(j, None)
        return 0

    lax.fori_loop(0, first_near, far_tile, 0)

    @pl.when(i >= 1)
    def _():
        attend(first_near, 0)

    for near in range(1, NEAR_TILES):
        attend(first_near + near, near)

    for h in range(N_HEADS):
        hs = slice(h * HEAD_DIM, (h + 1) * HEAD_DIM)
        o_ref[0, :, hs] = (acc_s[hs, :] / l_s[h:h + 1, :]).T.astype(BF16)


def _prompt_attn(rel_bias, qb, qib, kw, kb, vt, dbias, topk):
    b, t, _ = qb.shape
    nqi = N_IDX_HEADS * IDX_DIM
    assert TK + 1 >= FAR_DISTANCE and TQ % TK == 0
    return pl.pallas_call(
        functools.partial(_prompt_attn_kernel, topk=topk),
        grid=(b, t // TQ),
        in_specs=[pl.BlockSpec(memory_space=pltpu.SMEM),
                  pl.BlockSpec((1, TQ, D_MODEL), lambda bi, i: (bi, i, 0)),
                  pl.BlockSpec((1, TQ, nqi), lambda bi, i: (bi, i, 0)),
                  pl.BlockSpec((1, TQ, LANES), lambda bi, i: (bi, i, 0)),
                  pl.BlockSpec((1, t, LANES), lambda bi, i: (bi, 0, 0)),
                  pl.BlockSpec((1, t, D_MODEL), lambda bi, i: (bi, 0, 0)),
                  pl.BlockSpec((1, t // TK, D_MODEL, TK), lambda bi, i: (bi, 0, 0, 0)),
                  _const_spec((N_HEADS, NEAR_TILES * TK, TQ))],
        out_specs=pl.BlockSpec((1, TQ, D_MODEL), lambda bi, i: (bi, i, 0)),
        out_shape=jax.ShapeDtypeStruct((b, t, D_MODEL), BF16),
        scratch_shapes=[pltpu.VMEM((t // TK, TK, TQ), I32),
                        pltpu.VMEM((D_MODEL, TQ), F32),
                        pltpu.VMEM((N_HEADS, TQ), F32),
                        pltpu.VMEM((N_HEADS, TQ), F32)],
        compiler_params=_params(("parallel", "arbitrary")),
        name="prompt_attn",
    )(rel_bias, qb, qib, kw, kw, kb, vt, dbias)


def _sample_gather_kernel(pt_ref, relb_ref, q_ref, qi_ref, kw_ref, kn_ref, vn_ref, cki_hbm, ck_hbm, cv_hbm,
                          o_ref, ibuf, kg, vg, keys_s, idx_v, idx_sm, acc_s, isem, ksem, vsem, ssem,
                          *, t, past_len, topk, page_base):
    b = pl.program_id(0)
    n_idx_chunks = past_len // IDX_CHUNK
    n_pages = past_len // PAGE
    n_tiles = n_pages + 1
    sel_pages = topk // PAGE
    scale = HEAD_DIM ** -0.5
    nt_dims = (((1,), (1,)), ((), ()))

    def idx_copy(c, p):
        page = page_base + pt_ref[b, c * IDX_PAGES + p]
        return pltpu.make_async_copy(cki_hbm.at[page], ibuf.at[c * IDX_PAGES + p], isem.at[c])

    for c in range(n_idx_chunks):
        for p in range(IDX_PAGES):
            idx_copy(c, p).start()

    qi = qi_ref[0].astype(F32)
    qi_rows = jnp.concatenate([qi[:, h * IDX_DIM:(h + 1) * IDX_DIM] for h in range(N_IDX_HEADS)],
                              axis=0).astype(BF16)
    w = kw_ref[0][:, IDX_DIM:IDX_DIM + N_IDX_HEADS] * (N_IDX_HEADS ** -0.5) * (IDX_DIM ** -0.5)
    w_h = [w[:, h:h + 1] for h in range(N_IDX_HEADS)]

    def scores(ki_bf, transposed):
        if transposed:
            s = jnp.dot(qi_rows, ki_bf, preferred_element_type=F32)
        else:
            s = lax.dot_general(qi_rows, ki_bf, nt_dims, preferred_element_type=F32)
        score = jnp.maximum(s[0:t], 0.0) * w_h[0]
        for h in range(1, N_IDX_HEADS):
            score = score + jnp.maximum(s[h * t:(h + 1) * t], 0.0) * w_h[h]
        return score

    for c in range(n_idx_chunks):
        for p in range(IDX_PAGES):
            idx_copy(c, p).wait()
        ki_t = jnp.concatenate([ibuf[c * IDX_PAGES + p] for p in range(IDX_PAGES)], axis=1)
        key = _sortable(scores(ki_t.astype(BF16), True))
        for k in range(IDX_PAGES):
            keys_s[c * IDX_PAGES + k] = key[:, k * PAGE:(k + 1) * PAGE]

    ki_new = jnp.concatenate([kw_ref[0][:, 0:IDX_DIM], jnp.zeros((PAGE - t, IDX_DIM), F32)], axis=0)
    qrow = lax.broadcasted_iota(I32, (t, PAGE), 0)
    lane = lax.broadcasted_iota(I32, (t, PAGE), 1)
    keys_s[n_tiles - 1] = jnp.where(lane <= qrow, _sortable(scores(ki_new.astype(BF16), False)), INT_MIN)

    pos = (lax.broadcasted_iota(I32, (n_tiles, t, PAGE), 0) * PAGE
           + lax.broadcasted_iota(I32, (n_tiles, t, PAGE), 2))

    def count(pred):
        m = jnp.where(pred, 1, 0).astype(I32)
        part = m[0:n_pages]
        for group in (4, 4):
            part = jnp.sum(part.reshape(group, part.shape[0] // group, t, PAGE), axis=0)
        per_lane = jnp.sum(part, axis=0) + m[n_pages]
        return jnp.sum(per_lane, axis=1, keepdims=True)

    thr = _bisect(lambda x: count(keys_s[...] >= x[None]) >= topk, 32, INT_MIN, (t, 1))
    kall = keys_s[...]
    need = topk - count(kall > thr[None])
    thr1 = jnp.maximum(thr, INT_MIN + 1)[None]

    def tie_bound():
        return _bisect(lambda x: count((keys_s[...] == thr[None]) & (pos < x[None])) < need, 31, 0, (t, 1))

    has_tie = jnp.max(count(kall == thr1) - need) > 0
    xb = lax.cond(has_tie, tie_bound, lambda: jnp.full((t, 1), INT_MAX, I32))
    tied = jnp.where(pos <= xb[None], 1.0, 0.0)
    keys_s[...] = pltpu.bitcast(
        jnp.where(kall > thr1, 1.0, jnp.where(kall == thr1, tied, 0.0)).astype(F32), I32)

    sub_i = lax.broadcasted_iota(I32, (PAGE, PAGE), 0)
    lane_i = lax.broadcasted_iota(I32, (PAGE, PAGE), 1)
    upper = jnp.where(sub_i <= lane_i, 1.0, 0.0).astype(BF16)
    lower = jnp.where(lane_i <= sub_i, 1.0, 0.0).astype(BF16)
    ones = jnp.ones((PAGE, PAGE), BF16)
    slot = lax.broadcasted_iota(I32, (topk, PAGE), 0).astype(F32)
    tile_of_lane = lax.broadcasted_iota(I32, (topk, PAGE), 1).astype(F32)
    valid_rows = []
    for qq in range(t):
        sel = pltpu.bitcast(keys_s[0:n_pages, qq, :], F32).astype(BF16)
        within = jnp.dot(sel, upper, preferred_element_type=F32)
        cnt = jnp.broadcast_to(within[:, PAGE - 1:PAGE], (n_pages, PAGE)).astype(BF16)
        running = jnp.dot(lower, cnt, preferred_element_type=F32)
        running_t = running.T
        before = jnp.where(jnp.concatenate([running_t] * (topk // PAGE), axis=0) <= slot, 1.0, 0.0).astype(BF16)
        tile_r = jnp.dot(before, ones, preferred_element_type=F32)
        base_r = jnp.dot(before, cnt, preferred_element_type=F32)
        onehot = jnp.where(tile_r == tile_of_lane, 1.0, 0.0).astype(BF16)
        within_r = jnp.dot(onehot, within.astype(BF16), preferred_element_type=F32)
        lane_r = jnp.dot(jnp.where(within_r <= slot - base_r, 1.0, 0.0).astype(BF16), ones,
                         preferred_element_type=F32)
        valid = slot < running[n_pages - 1:n_pages, :]
        pos_r = jnp.where(valid, tile_r * PAGE + lane_r, 0.0)
        idx_v[qq:qq + 1, :] = pos_r.T[0:1, :].astype(I32)
        valid_rows.append(jnp.where(valid, 1.0, 0.0).T[0:1, :])
    valid_g = jnp.concatenate(valid_rows, axis=0)

    to_smem = pltpu.make_async_copy(idx_v, idx_sm, ssem)
    to_smem.start()
    to_smem.wait()

    def issue(j, _):
        dst_page = lax.shift_right_logical(j, 7)
        dst_row = pl.multiple_of((j & (PAGE - 1)) * N_HEADS, N_HEADS)
        for qq in range(t):
            p_key = idx_sm[qq, j]
            page = page_base + pt_ref[b, lax.shift_right_logical(p_key, 7)]
            src_row = pl.multiple_of((p_key & (PAGE - 1)) * N_HEADS, N_HEADS)
            dst = (qq * sel_pages + dst_page, pl.ds(dst_row, N_HEADS))
            pltpu.make_async_copy(ck_hbm.at[page, pl.ds(src_row, N_HEADS)], kg.at[dst], ksem).start()
            pltpu.make_async_copy(cv_hbm.at[page, pl.ds(src_row, N_HEADS)], vg.at[dst], vsem).start()
        return 0

    lax.fori_loop(0, topk, issue, 0)
    pltpu.make_async_copy(ck_hbm.at[pl.ds(0, t * sel_pages)], kg, ksem).wait()
    pltpu.make_async_copy(cv_hbm.at[pl.ds(0, t * sel_pages)], vg, vsem).wait()

    n_new = LANES
    n_g = topk * N_HEADS

    def expand(x):
        per_group = LANES // N_HEADS
        lane_slot = lax.broadcasted_iota(I32, (t, LANES), 1) >> 3
        out = []
        for c in range((n_g + n_new) // LANES):
            src = x[:, (c * per_group // LANES) * LANES:(c * per_group // LANES + 1) * LANES]
            out.append(jnp.take_along_axis(src, (c * per_group) % LANES + lane_slot, axis=1))
        return jnp.concatenate(out, axis=1)

    qpos = past_len + lax.broadcasted_iota(I32, (t, topk + PAGE), 0)
    new_lane = lax.broadcasted_iota(I32, (t, PAGE), 1)
    key_pos = jnp.concatenate([idx_v[...], past_len + new_lane], axis=1)
    sel_new = pltpu.bitcast(keys_s[n_tiles - 1], F32)
    take = jnp.concatenate([valid_g, sel_new], axis=1)
    bucket_x = expand(_t5_bucket(qpos - key_pos).astype(F32))[:, 0:n_g + n_new]
    take_x = expand(take)[:, 0:n_g + n_new]
    head = lax.broadcasted_iota(I32, (t, n_g + n_new), 1) & (N_HEADS - 1)
    in_range = lax.broadcasted_iota(I32, (t, n_g + n_new), 1) < n_g + t * N_HEADS
    bias = jnp.zeros((t, n_g + n_new), F32)
    for h in range(N_HEADS):
        bias = jnp.where(head == h, _bias_of_bucket(bucket_x, relb_ref, h), bias)
    amask = jnp.where((take_x > 0.5) & in_range, 0.0, NEG_BIG)

    qf = q_ref[0].astype(F32)
    pad = jnp.zeros((LANES - t * N_HEADS, HEAD_DIM), F32)
    k_new = jnp.concatenate([kn_ref[0], pad], axis=0).astype(BF16)
    v_new = jnp.concatenate([vn_ref[0], pad], axis=0).astype(BF16)

    def own_head(n):
        return (lax.broadcasted_iota(I32, (N_HEADS, n), 0)
                == (lax.broadcasted_iota(I32, (N_HEADS, n), 1) & (N_HEADS - 1)))

    lg_rows = []
    for qq in range(t):
        q_heads = jnp.concatenate([qf[qq:qq + 1, h * HEAD_DIM:(h + 1) * HEAD_DIM] for h in range(N_HEADS)],
                                  axis=0).astype(BF16)
        parts = []
        for x_bf in [kg[qq * sel_pages + s].astype(BF16) for s in range(sel_pages)] + [k_new]:
            r = lax.dot_general(q_heads, x_bf, nt_dims, preferred_element_type=F32)
            parts.append(jnp.sum(jnp.where(own_head(r.shape[1]), r, 0.0), axis=0, keepdims=True))
        lg_rows.append(jnp.concatenate(parts, axis=1))
    lg = jnp.concatenate(lg_rows, axis=0) * scale + bias + amask

    def per_head_allreduce(x, op):
        for shift in (8, 16, 32, 64):
            x = op(x, pltpu.roll(x, shift, axis=1))
        return x

    groups = (n_g + n_new) // LANES
    m_run = lg[:, 0:LANES]
    for s in range(1, groups):
        m_run = jnp.maximum(m_run, lg[:, s * LANES:(s + 1) * LANES])
    m_fin = per_head_allreduce(m_run, jnp.maximum)
    pr = jnp.exp(lg - jnp.concatenate([m_fin] * groups, axis=1))
    l_run = pr[:, 0:LANES]
    for s in range(1, groups):
        l_run = l_run + pr[:, s * LANES:(s + 1) * LANES]
    l_fin = per_head_allreduce(l_run, jnp.add)

    for qq in range(t):
        acc = jnp.zeros((N_HEADS, HEAD_DIM), F32)
        lane0 = 0
        for x_bf in [vg[qq * sel_pages + s].astype(BF16) for s in range(sel_pages)] + [v_new]:
            n = x_bf.shape[0]
            p_heads = jnp.where(own_head(n), jnp.broadcast_to(pr[qq:qq + 1, lane0:lane0 + n], (N_HEADS, n)), 0.0)
            acc = acc + jnp.dot(p_heads.astype(BF16), x_bf, preferred_element_type=F32)
            lane0 += n
        acc_s[qq * N_HEADS:(qq + 1) * N_HEADS, :] = acc
    for h in range(N_HEADS):
        o_ref[0, :, h * HEAD_DIM:(h + 1) * HEAD_DIM] = (
            acc_s[pl.ds(h, t, stride=N_HEADS), :] / l_fin[:, h:h + 1]).astype(BF16)


def _sample_gather_attn(page_table, rel_bias, qb, qib, kw, k_new, v_new, cki, ck, cv, topk, page_base):
    b, t, _ = qb.shape
    n_pages = page_table.shape[1]
    past_len = n_pages * PAGE
    nqi = N_IDX_HEADS * IDX_DIM
    assert t == SUBLANES and n_pages == PAGE and topk % PAGE == 0 and n_pages % IDX_PAGES == 0
    per_b = lambda bi, pt: (bi, 0, 0)
    grid_spec = pltpu.PrefetchScalarGridSpec(
        num_scalar_prefetch=1,
        grid=(b,),
        in_specs=[pl.BlockSpec(memory_space=pltpu.SMEM),
                  pl.BlockSpec((1, t, D_MODEL), per_b),
                  pl.BlockSpec((1, t, nqi), per_b),
                  pl.BlockSpec((1, t, LANES), per_b),
                  pl.BlockSpec((1, t * N_HEADS, HEAD_DIM), per_b),
                  pl.BlockSpec((1, t * N_HEADS, HEAD_DIM), per_b),
                  pl.BlockSpec(memory_space=pl.ANY),
                  pl.BlockSpec(memory_space=pl.ANY),
                  pl.BlockSpec(memory_space=pl.ANY)],
        out_specs=pl.BlockSpec((1, t, D_MODEL), per_b),
        scratch_shapes=[pltpu.VMEM((n_pages, IDX_DIM, PAGE), F32),
                        pltpu.VMEM((t * topk // PAGE, PAGE_ROWS, HEAD_DIM), F32),
                        pltpu.VMEM((t * topk // PAGE, PAGE_ROWS, HEAD_DIM), F32),
                        pltpu.VMEM((n_pages + 1, t, PAGE), I32),
                        pltpu.VMEM((t, topk), I32),
                        pltpu.SMEM((t, topk), I32),
                        pltpu.VMEM((t * N_HEADS, HEAD_DIM), F32),
                        pltpu.SemaphoreType.DMA((n_pages // IDX_PAGES,)),
                        pltpu.SemaphoreType.DMA(()),
                        pltpu.SemaphoreType.DMA(()),
                        pltpu.SemaphoreType.DMA(())])
    return pl.pallas_call(
        functools.partial(_sample_gather_kernel, t=t, past_len=past_len, topk=topk, page_base=page_base),
        grid_spec=grid_spec,
        out_shape=jax.ShapeDtypeStruct((b, t, D_MODEL), BF16),
        compiler_params=_params(("arbitrary",)),
        name="sample_attn",
    )(page_table, rel_bias, qb, qib, kw, k_new, v_new, cki, ck, cv)


def kernel(x_prompt, x_sample, state_conv_a, state_conv_b, cache_k, cache_v, cache_kidx, page_table,
           norm_conv, w_in_conv, conv_a_w, conv_b_w, conv_b_bias, ln_b_gain, ln_b_bias, w_out_conv,
           norm_attn, w_in_attn, w_out_attn, rel_bias, norm_mlp, w_up, w_down, norm_final):
    depth = norm_mlp.shape[0]
    n_pool = cache_k.shape[1]
    past_len = page_table.shape[1] * PAGE
    nqkv = 3 * D_MODEL
    nqi = N_IDX_HEADS * IDX_DIM
    row = lambda a: a.reshape(1, -1)

    cki_flat = jnp.swapaxes(cache_kidx, 2, 3).reshape(-1, IDX_DIM, PAGE)
    ck_flat = cache_k.reshape(-1, PAGE_ROWS, HEAD_DIM)
    cv_flat = cache_v.reshape(-1, PAGE_ROWS, HEAD_DIM)
    dbias = _prompt_bias(rel_bias)

    def run(x, sample):
        b, t, _ = x.shape
        m = b * t
        tm = min(512, m)
        x2d = x.reshape(m, D_MODEL)
        new_a, new_b, new_k, new_v, new_ki = [], [], [], [], []
        for layer in range(depth):
            i = layer // 2
            last = layer == depth - 1
            if layer % 2 == 0:
                if sample:
                    buf_a, buf_b = state_conv_a[i], state_conv_b[i]
                else:
                    buf_a = jnp.zeros((b, CONV_A_WIDTH - 1, D_A), F32)
                    buf_b = jnp.zeros((b, CONV_B_WIDTH - 1, D_B), F32)
                sa = jnp.pad(buf_a, ((0, 0), (A_HALO - (CONV_A_WIDTH - 1), 0), (0, 0)))
                sb = jnp.pad(buf_b, ((0, 0), (B_HALO - (CONV_B_WIDTH - 1), 0), (0, 0)))
                gb, cx, u = _conv_in(x2d, row(norm_conv[i]), w_in_conv[i].astype(BF16), tm)
                tt = min(512, t)
                y, na, nb = _conv(gb.reshape(b, t, D_A), cx.reshape(b, t, D_A), u.reshape(b, t, D_B), sa, sb,
                                  conv_a_w[i], conv_b_w[i], row(conv_b_bias[i]), row(ln_b_gain[i]),
                                  row(ln_b_bias[i]), tt, min(64, tt))
                new_a.append(na)
                new_b.append(nb)
                mix = y.reshape(m, D_MODEL)
                w_out = w_out_conv[i]
            else:
                w_in = w_in_attn[i]
                wkw = jnp.pad(w_in[:, nqkv + nqi:], ((0, 0), (0, LANES - IDX_DIM - N_IDX_HEADS)))
                wvt = w_in[:, 2 * D_MODEL:nqkv].T.astype(BF16)
                outs = _attn_in(x2d, row(norm_attn[i]), w_in[:, :nqkv].astype(BF16), wvt,
                                w_in[:, nqkv:nqkv + nqi].astype(BF16), wkw.astype(BF16), tm, t, not sample)
                qb, k, v, qib, kw = outs[:5]
                shp = lambda a: a.reshape(b, t, a.shape[-1])
                if sample:
                    topk = min(TOPK_MAX, (past_len + t) // 4)
                    o = _sample_gather_attn(page_table, rel_bias, shp(qb), shp(qib), shp(kw),
                                            k.reshape(b, t * N_HEADS, HEAD_DIM),
                                            v.reshape(b, t * N_HEADS, HEAD_DIM),
                                            cki_flat, ck_flat, cv_flat, topk, i * n_pool)
                else:
                    topk = min(TOPK_MAX, t // 4)
                    kb, vt = outs[5:]
                    o = _prompt_attn(rel_bias, shp(qb), shp(qib), shp(kw), shp(kb), vt, dbias, topk)
                new_k.append(k.reshape(b, t, N_HEADS, HEAD_DIM))
                new_v.append(v.reshape(b, t, N_HEADS, HEAD_DIM))
                new_ki.append(kw[:, :IDX_DIM].reshape(b, t, IDX_DIM))
                mix = o.reshape(m, D_MODEL)
                w_out = w_out_attn[i]
            x2d = _out_mlp(mix, x2d, w_out.astype(BF16), row(norm_mlp[layer]), w_up[layer].astype(BF16),
                           w_down[layer].astype(BF16), row(norm_final), tm, last)
        return (x2d.reshape(b, t, D_MODEL), jnp.stack(new_a), jnp.stack(new_b),
                jnp.stack(new_k), jnp.stack(new_v), jnp.stack(new_ki))

    y_p, ca_p, cb_p, k_p, v_p, ki_p = run(x_prompt, False)
    y_s, ca_s, cb_s, k_s, v_s, ki_s = run(x_sample, True)
    return (y_p, y_s, ca_p, cb_p, k_p, v_p, ki_p, ca_s, cb_s, k_s, v_s, ki_s)
```

```python
import functools
import math

import jax
import jax.numpy as jnp
from jax import lax
from jax.experimental import pallas as pl
from jax.experimental.pallas import tpu as pltpu

F32 = jnp.float32
BF16 = jnp.bfloat16
I32 = jnp.int32

D_MODEL = 1024
D_A = 512
D_B = 512
CONV_A_WIDTH = 3
CONV_B_WIDTH = 31
N_HEADS = 8
HEAD_DIM = 128
N_IDX_HEADS = 8
IDX_DIM = 64
TOPK_MAX = 256
NUM_BUCKETS = 32
MAX_DISTANCE = 128
D_FF = 4096
EPS = 1e-6
PAGE = 128
LANES = 128
SUBLANES = 8

INT_MIN = -(2 ** 31)
INT_MAX = 2 ** 31 - 1
NEG_BIG = -1e30

VMEM_LIMIT_BYTES = 56 * 1024 * 1024

A_HALO = 8
B_HALO = 32

TQ = 256
TK = 128
NEAR_TILES = TQ // TK + 1
def _first_far_distance():
    exact = NUM_BUCKETS // 2
    d = exact
    while exact + int(math.log(d / exact) / math.log(MAX_DISTANCE / exact) * (NUM_BUCKETS - exact)) < NUM_BUCKETS - 1:
        d += 1
    return d


FAR_DISTANCE = _first_far_distance()
LOG2E = math.log2(math.e)
QK_SCALE = (HEAD_DIM ** -0.5) * LOG2E
IDX_PAGES = 16
IDX_CHUNK = IDX_PAGES * PAGE
PAGE_ROWS = PAGE * N_HEADS


def _params(sem):
    return pltpu.CompilerParams(dimension_semantics=sem, vmem_limit_bytes=VMEM_LIMIT_BYTES)


def _const_spec(shape):
    nd = len(shape)
    return pl.BlockSpec(shape, lambda *_: (0,) * nd, pipeline_mode=pl.Buffered(1))


def _rms_bf16(x, g):
    ms = jnp.mean(x * x, axis=-1, keepdims=True)
    return ((x * lax.rsqrt(ms + EPS)) * g).astype(BF16)


def _sigmoid(x):
    return 1.0 / (1.0 + jnp.exp(-x))


def _conv_in_kernel(x_ref, g_ref, w_ref, gb_ref, cx_ref, u_ref):
    xn = _rms_bf16(x_ref[...], g_ref[...])
    proj = jnp.dot(xn, w_ref[...], preferred_element_type=F32)
    gb_ref[...] = proj[:, :D_A]
    cx_ref[...] = proj[:, D_A:2 * D_A] * proj[:, 2 * D_A:3 * D_A]
    u_ref[...] = proj[:, 3 * D_A:3 * D_A + D_B] * _sigmoid(proj[:, 3 * D_A + D_B:])


def _conv_in(x2d, g, w_bf, tm):
    m = x2d.shape[0]
    n = w_bf.shape[1]
    out = jax.ShapeDtypeStruct((m, D_A), F32)
    return pl.pallas_call(
        _conv_in_kernel,
        grid=(m // tm,),
        in_specs=[pl.BlockSpec((tm, D_MODEL), lambda i: (i, 0)),
                  _const_spec((1, D_MODEL)),
                  _const_spec((D_MODEL, n))],
        out_specs=[pl.BlockSpec((tm, D_A), lambda i: (i, 0))] * 3,
        out_shape=[out, out, out],
        compiler_params=_params(("parallel",)),
        name="conv_in",
    )(x2d, g, w_bf)


def _conv_kernel(gb_ref, cx_ref, u_ref, sa_ref, sb_ref, wa_ref, wb_ref, bias_ref, lng_ref, lnb_ref,
                 y_ref, na_ref, nb_ref, xa_s, xb_s, sh_s, *, tt, rc):
    t = pl.program_id(1)

    @pl.when(t == 0)
    def _():
        xa_s[0:A_HALO, :] = sa_ref[0]
        xb_s[0:B_HALO, :] = sb_ref[0]

    @pl.when(t > 0)
    def _():
        xa_s[0:A_HALO, :] = xa_s[tt:tt + A_HALO, :]
        xb_s[0:B_HALO, :] = xb_s[tt:tt + B_HALO, :]

    xa_s[A_HALO:A_HALO + tt, :] = cx_ref[0]
    xb_s[B_HALO:B_HALO + tt, :] = u_ref[0]

    bias = bias_ref[...]
    lng = lng_ref[...]
    lnb = lnb_ref[...]
    a0 = A_HALO - (CONV_A_WIDTH - 1)
    b0 = B_HALO - (CONV_B_WIDTH - 1)
    for c in range(tt // rc):
        r0 = c * rc
        ya = wa_ref[0:1, :] * xa_s[a0 + r0:a0 + r0 + rc, :]
        for j in range(1, CONV_A_WIDTH):
            ya = ya + wa_ref[j:j + 1, :] * xa_s[a0 + j + r0:a0 + j + r0 + rc, :]
        y_ref[0, r0:r0 + rc, 0:D_A] = gb_ref[0, r0:r0 + rc, :] * ya

        yb = None
        for shift in range(SUBLANES):
            offs = [o for o in range(b0, b0 + CONV_B_WIDTH) if o % SUBLANES == shift]
            if not offs:
                continue
            rows_needed = offs[-1] - shift + rc
            sh_s[shift, 0:rows_needed, :] = xb_s[r0 + shift:r0 + shift + rows_needed, :]
            for o in offs:
                term = wb_ref[o - b0:o - b0 + 1, :] * sh_s[shift, o - shift:o - shift + rc, :]
                yb = term if yb is None else yb + term
        yb = yb + bias
        mu = jnp.mean(yb, axis=-1, keepdims=True)
        var = jnp.mean(jnp.square(yb - mu), axis=-1, keepdims=True)
        z = (yb - mu) * lax.rsqrt(var + EPS) * lng + lnb
        y_ref[0, r0:r0 + rc, D_A:D_A + D_B] = z * _sigmoid(z)

    na_ref[0] = xa_s[tt + A_HALO - (CONV_A_WIDTH - 1):tt + A_HALO, :]
    nb_ref[0] = xb_s[tt + B_HALO - (CONV_B_WIDTH - 1):tt + B_HALO, :]


def _conv(gb, cx, u, sa, sb, wa, wb, bias, lng, lnb, tt, rc):
    b, t, _ = cx.shape
    row = lambda i, j: (i, j, 0)
    per_b = lambda i, j: (i, 0, 0)
    return pl.pallas_call(
        functools.partial(_conv_kernel, tt=tt, rc=rc),
        grid=(b, t // tt),
        in_specs=[pl.BlockSpec((1, tt, D_A), row),
                  pl.BlockSpec((1, tt, D_A), row),
                  pl.BlockSpec((1, tt, D_B), row),
                  pl.BlockSpec((1, A_HALO, D_A), per_b),
                  pl.BlockSpec((1, B_HALO, D_B), per_b),
                  pl.BlockSpec((CONV_A_WIDTH, D_A), lambda i, j: (0, 0)),
                  pl.BlockSpec((CONV_B_WIDTH, D_B), lambda i, j: (0, 0)),
                  pl.BlockSpec((1, D_B), lambda i, j: (0, 0)),
                  pl.BlockSpec((1, D_B), lambda i, j: (0, 0)),
                  pl.BlockSpec((1, D_B), lambda i, j: (0, 0))],
        out_specs=[pl.BlockSpec((1, tt, D_MODEL), row),
                   pl.BlockSpec((1, CONV_A_WIDTH - 1, D_A), per_b),
                   pl.BlockSpec((1, CONV_B_WIDTH - 1, D_B), per_b)],
        out_shape=[jax.ShapeDtypeStruct((b, t, D_MODEL), F32),
                   jax.ShapeDtypeStruct((b, CONV_A_WIDTH - 1, D_A), F32),
                   jax.ShapeDtypeStruct((b, CONV_B_WIDTH - 1, D_B), F32)],
        scratch_shapes=[pltpu.VMEM((A_HALO + tt, D_A), F32),
                        pltpu.VMEM((B_HALO + tt, D_B), F32),
                        pltpu.VMEM((SUBLANES, B_HALO + rc, D_B), F32)],
        compiler_params=_params(("parallel", "arbitrary")),
        name="conv_mix",
    )(gb, cx, u, sa, sb, wa, wb, bias, lng, lnb)


def _out_mlp_kernel(a_ref, res_ref, wo_ref, g_ref, wu_ref, wd_ref, gf_ref, o_ref, *, final_norm, ff_chunk):
    x1 = res_ref[...] + jnp.dot(a_ref[...].astype(BF16), wo_ref[...], preferred_element_type=F32)
    xn = _rms_bf16(x1, g_ref[...])
    acc = x1
    for c in range(D_FF // ff_chunk):
        h = jnp.dot(xn, wu_ref[:, c * ff_chunk:(c + 1) * ff_chunk], preferred_element_type=F32)
        h = jnp.square(jnp.maximum(h, 0.0)).astype(BF16)
        acc = acc + jnp.dot(h, wd_ref[c * ff_chunk:(c + 1) * ff_chunk, :], preferred_element_type=F32)
    if final_norm:
        ms = jnp.mean(acc * acc, axis=-1, keepdims=True)
        acc = (acc * lax.rsqrt(ms + EPS)) * gf_ref[...]
    o_ref[...] = acc


def _out_mlp(a2d, res2d, wo_bf, g, wu_bf, wd_bf, gf, tm, final_norm):
    m = res2d.shape[0]
    row = lambda i: (i, 0)
    return pl.pallas_call(
        functools.partial(_out_mlp_kernel, final_norm=final_norm, ff_chunk=1024),
        grid=(m // tm,),
        in_specs=[pl.BlockSpec((tm, D_MODEL), row),
                  pl.BlockSpec((tm, D_MODEL), row),
                  _const_spec((D_MODEL, D_MODEL)),
                  _const_spec((1, D_MODEL)),
                  _const_spec((D_MODEL, D_FF)),
                  _const_spec((D_FF, D_MODEL)),
                  _const_spec((1, D_MODEL))],
        out_specs=pl.BlockSpec((tm, D_MODEL), row),
        out_shape=jax.ShapeDtypeStruct((m, D_MODEL), F32),
        compiler_params=_params(("parallel",)),
        name="out_mlp",
    )(a2d, res2d, wo_bf, g, wu_bf, wd_bf, gf)


def _attn_in_kernel(*refs, prompt, tm):
    if prompt:
        (x_ref, g_ref, wqkv_ref, wvt_ref, wqi_ref, wkw_ref,
         qb_ref, k_ref, v_ref, qib_ref, kw_ref, kb_ref, vt_ref) = refs
    else:
        x_ref, g_ref, wqkv_ref, wqi_ref, wkw_ref, qb_ref, k_ref, v_ref, qib_ref, kw_ref = refs
    xn = _rms_bf16(x_ref[...], g_ref[...])
    q = jnp.dot(xn, wqkv_ref[:, 0:D_MODEL], preferred_element_type=F32)
    qb_ref[...] = (q * QK_SCALE if prompt else q).astype(BF16)
    k = jnp.dot(xn, wqkv_ref[:, D_MODEL:2 * D_MODEL], preferred_element_type=F32)
    v = jnp.dot(xn, wqkv_ref[:, 2 * D_MODEL:3 * D_MODEL], preferred_element_type=F32)
    for h in range(N_HEADS):
        k_ref[:, h, :] = k[:, h * HEAD_DIM:(h + 1) * HEAD_DIM]
        v_ref[:, h, :] = v[:, h * HEAD_DIM:(h + 1) * HEAD_DIM]
    qib_ref[...] = jnp.dot(xn, wqi_ref[...], preferred_element_type=F32).astype(BF16)
    kw_ref[...] = jnp.dot(xn, wkw_ref[...], preferred_element_type=F32)
    if prompt:
        kb_ref[...] = k.astype(BF16)
        vt = lax.dot_general(wvt_ref[...], xn, (((1,), (1,)), ((), ())),
                             preferred_element_type=F32).astype(BF16)
        for s in range(tm // TK):
            vt_ref[0, s] = vt[:, s * TK:(s + 1) * TK]


def _attn_in(x2d, g, wqkv_bf, wvt_bf, wqi_bf, wkw_bf, tm, seq_len, prompt):
    m = x2d.shape[0]
    row = lambda i: (i, 0)
    row3 = lambda i: (i, 0, 0)
    nqi = N_IDX_HEADS * IDX_DIM
    in_specs = [pl.BlockSpec((tm, D_MODEL), row),
                _const_spec((1, D_MODEL)),
                _const_spec((D_MODEL, 3 * D_MODEL))]
    args = [x2d, g, wqkv_bf]
    if prompt:
        in_specs.append(_const_spec((D_MODEL, D_MODEL)))
        args.append(wvt_bf)
    in_specs += [_const_spec((D_MODEL, nqi)), _const_spec((D_MODEL, LANES))]
    args += [wqi_bf, wkw_bf]
    out_specs = [pl.BlockSpec((tm, D_MODEL), row),
                 pl.BlockSpec((tm, N_HEADS, HEAD_DIM), row3),
                 pl.BlockSpec((tm, N_HEADS, HEAD_DIM), row3),
                 pl.BlockSpec((tm, nqi), row),
                 pl.BlockSpec((tm, LANES), row)]
    out_shape = [jax.ShapeDtypeStruct((m, D_MODEL), BF16),
                 jax.ShapeDtypeStruct((m, N_HEADS, HEAD_DIM), F32),
                 jax.ShapeDtypeStruct((m, N_HEADS, HEAD_DIM), F32),
                 jax.ShapeDtypeStruct((m, nqi), BF16),
                 jax.ShapeDtypeStruct((m, LANES), F32)]
    if prompt:
        steps_per_seq = seq_len // tm
        out_specs += [pl.BlockSpec((tm, D_MODEL), row),
                      pl.BlockSpec((1, tm // TK, D_MODEL, TK),
                                   lambda i: (i // steps_per_seq, i % steps_per_seq, 0, 0))]
        out_shape += [jax.ShapeDtypeStruct((m, D_MODEL), BF16),
                      jax.ShapeDtypeStruct((m // seq_len, seq_len // TK, D_MODEL, TK), BF16)]
    return pl.pallas_call(
        functools.partial(_attn_in_kernel, prompt=prompt, tm=tm),
        grid=(m // tm,),
        in_specs=in_specs,
        out_specs=out_specs,
        out_shape=out_shape,
        compiler_params=_params(("parallel",)),
        name="attn_in",
    )(*args)


def _t5_bucket(d):
    n = jnp.maximum(d, 0)
    max_exact = NUM_BUCKETS // 2
    nf = jnp.maximum(n, 1).astype(F32)
    large = max_exact + (jnp.log(nf / max_exact) / math.log(MAX_DISTANCE / max_exact)
                         * (NUM_BUCKETS - max_exact)).astype(I32)
    large = jnp.minimum(large, NUM_BUCKETS - 1)
    return jnp.where(n < max_exact, n, large)


def _bias_of_bucket(bucket, relb_ref, h):
    val = jnp.full(bucket.shape, relb_ref[NUM_BUCKETS - 1, h], F32)
    for bk in range(NUM_BUCKETS - 2, -1, -1):
        val = jnp.where(bucket == bk, relb_ref[bk, h], val)
    return val


def _prompt_bias_kernel(relb_ref, o_ref):
    h = pl.program_id(0)
    c = lax.broadcasted_iota(I32, (NEAR_TILES * TK, TQ), 0)
    r = lax.broadcasted_iota(I32, (NEAR_TILES * TK, TQ), 1)
    o_ref[0] = _bias_of_bucket(_t5_bucket(r + TK - c), relb_ref, h) * LOG2E


def _prompt_bias(rel_bias):
    return pl.pallas_call(
        _prompt_bias_kernel,
        grid=(N_HEADS,),
        in_specs=[pl.BlockSpec(memory_space=pltpu.SMEM)],
        out_specs=pl.BlockSpec((1, NEAR_TILES * TK, TQ), lambda h: (h, 0, 0)),
        out_shape=jax.ShapeDtypeStruct((N_HEADS, NEAR_TILES * TK, TQ), F32),
        compiler_params=_params(("parallel",)),
        name="prompt_bias",
    )(rel_bias)


def _sortable(score):
    bits = pltpu.bitcast(score, I32)
    return jnp.where(bits < 0, bits ^ INT_MAX, bits)


def _bisect(count_fn, n_bits, init, shape):
    def body(it, cur):
        trial = cur ^ lax.shift_left(jnp.int32(1), jnp.asarray(n_bits - 1 - it, I32))
        return jnp.where(count_fn(trial), trial, cur)
    return lax.fori_loop(0, n_bits, body, jnp.full(shape, init, I32))


def _prompt_attn_kernel(relb_ref, q_ref, qi_ref, kwq_ref, kw_ref, k_ref, vt_ref, db_ref, o_ref,
                        keys_s, acc_s, m_s, l_s, *, topk):
    i = pl.program_id(1)
    tiles_per_q = TQ // TK
    nt = (i + 1) * tiles_per_q
    nt_dims = (((1,), (1,)), ((), ()))
    row = (1, TQ)

    qi = qi_ref[0]
    qi_h = [qi[:, h * IDX_DIM:(h + 1) * IDX_DIM] for h in range(N_IDX_HEADS)]
    kwq_t = kwq_ref[0].T
    w_scale = (N_IDX_HEADS ** -0.5) * (IDX_DIM ** -0.5)
    w_h = [kwq_t[IDX_DIM + h:IDX_DIM + h + 1, :] * w_scale for h in range(N_IDX_HEADS)]
    krow = lax.broadcasted_iota(I32, (TK, TQ), 0)
    qpos = i * TQ + lax.broadcasted_iota(I32, (TK, TQ), 1)

    def score_chunk(jj, _):
        r0 = pl.multiple_of(jj * TQ, TQ)
        ki_t = kw_ref[0, pl.ds(r0, TQ), 0:IDX_DIM].astype(BF16)
        score = jnp.zeros((TQ, TQ), F32)
        for h in range(N_IDX_HEADS):
            s = lax.dot_general(ki_t, qi_h[h], nt_dims, preferred_element_type=F32)
            score = score + jnp.maximum(s, 0.0) * w_h[h]
        key = _sortable(score)
        for s in range(tiles_per_q):
            j = jj * tiles_per_q + s
            keys_s[j] = jnp.where(j * TK + krow <= qpos, key[s * TK:(s + 1) * TK, :], INT_MIN)
        return 0

    lax.fori_loop(0, i + 1, score_chunk, 0)

    def count(pred_fn):
        def body(j, c):
            m = jnp.where(pred_fn(keys_s[j], j), 1, 0).astype(I32)
            return c + jnp.sum(m.reshape(TK // 8, 8, TQ), axis=0)
        c = lax.fori_loop(0, nt, body, jnp.zeros((8, TQ), I32))
        return jnp.sum(c, axis=0, keepdims=True)

    thr = _bisect(lambda x: count(lambda kt, j: kt >= x) >= topk, 32, INT_MIN, row)
    n_gt = count(lambda kt, j: kt > thr)
    thr1 = jnp.maximum(thr, INT_MIN + 1)
    n_eq = count(lambda kt, j: kt == thr1)
    need = topk - n_gt

    def tie_bound():
        return _bisect(lambda x: count(lambda kt, j: jnp.where(kt == thr, j * TK + krow, INT_MAX) < x) < need,
                       31, 0, row)

    has_tie = jnp.max(n_eq - need) > 0
    xb = lax.cond(has_tie, tie_bound, lambda: jnp.full(row, INT_MAX, I32))

    q = q_ref[0]
    q_h = [q[:, h * HEAD_DIM:(h + 1) * HEAD_DIM] for h in range(N_HEADS)]
    acc_s[...] = jnp.zeros_like(acc_s)
    m_s[...] = jnp.full_like(m_s, NEG_BIG)
    l_s[...] = jnp.zeros_like(l_s)

    def attend(j, near):
        r0 = pl.multiple_of(j * TK, TK)
        kt = keys_s[j]
        tied = jnp.where(j * TK + krow <= xb, 0.0, -jnp.inf)
        amask = jnp.where(kt > thr1, 0.0, jnp.where(kt == thr1, tied, -jnp.inf))
        for h in range(N_HEADS):
            hs = slice(h * HEAD_DIM, (h + 1) * HEAD_DIM)
            k_t = k_ref[0, pl.ds(r0, TK), hs]
            s = lax.dot_general(k_t, q_h[h], nt_dims, preferred_element_type=F32)
            m_old = m_s[h:h + 1, :]
            if near is None:
                c = relb_ref[NUM_BUCKETS - 1, h] * LOG2E
                lg = s + amask
                m_new = jnp.maximum(m_old, jnp.max(lg, axis=0, keepdims=True) + c)
                p = jnp.exp2(lg - (m_new - c))
            else:
                lg = s + (amask + db_ref[h, near * TK:(near + 1) * TK, :])
                m_new = jnp.maximum(m_old, jnp.max(lg, axis=0, keepdims=True))
                p = jnp.exp2(lg - m_new)
            alpha = jnp.exp2(m_old - m_new)
            l_s[h:h + 1, :] = alpha * l_s[h:h + 1, :] + jnp.sum(p, axis=0, keepdims=True)
            m_s[h:h + 1, :] = m_new
            pv = jnp.dot(vt_ref[0, j, hs, :], p.astype(BF16), preferred_element_type=F32)
            acc_s[hs, :] = alpha * acc_s[hs, :] + pv

    first_near = i * tiles_per_q - 1

    def far_pair(jj, _):
        attend(2 * jj, None)
        attend(2 * jj + 1, None)
        return 0

    lax.fori_loop(0, lax.shift_right_logical(jnp.maximum(first_near, 0), 1), far_pair, 0)

    @pl.when(jnp.logical_and(i >= 1, (first_near & 1) == 1))
    def _():
        attend(first_near - 1, None)

    @pl.when(i >= 1)
    def _():
        attend(first_near, 0)

    for near in range(1, NEAR_TILES):
        attend(first_near + near, near)

    for h in range(N_HEADS):
        hs = slice(h * HEAD_DIM, (h + 1) * HEAD_DIM)
        o_ref[0, :, hs] = (acc_s[hs, :] / l_s[h:h + 1, :]).T.astype(BF16)


def _prompt_attn(rel_bias, qb, qib, kw, kb, vt, dbias, topk):
    b, t, _ = qb.shape
    nqi = N_IDX_HEADS * IDX_DIM
    assert TK + 1 >= FAR_DISTANCE and TQ % TK == 0
    return pl.pallas_call(
        functools.partial(_prompt_attn_kernel, topk=topk),
        grid=(b, t // TQ),
        in_specs=[pl.BlockSpec(memory_space=pltpu.SMEM),
                  pl.BlockSpec((1, TQ, D_MODEL), lambda bi, i: (bi, i, 0)),
                  pl.BlockSpec((1, TQ, nqi), lambda bi, i: (bi, i, 0)),
                  pl.BlockSpec((1, TQ, LANES), lambda bi, i: (bi, i, 0)),
                  pl.BlockSpec((1, t, LANES), lambda bi, i: (bi, 0, 0)),
                  pl.BlockSpec((1, t, D_MODEL), lambda bi, i: (bi, 0, 0)),
                  pl.BlockSpec((1, t // TK, D_MODEL, TK), lambda bi, i: (bi, 0, 0, 0)),
                  _const_spec((N_HEADS, NEAR_TILES * TK, TQ))],
        out_specs=pl.BlockSpec((1, TQ, D_MODEL), lambda bi, i: (bi, i, 0)),
        out_shape=jax.ShapeDtypeStruct((b, t, D_MODEL), BF16),
        scratch_shapes=[pltpu.VMEM((t // TK, TK, TQ), I32),
                        pltpu.VMEM((D_MODEL, TQ), F32),
                        pltpu.VMEM((N_HEADS, TQ), F32),
                        pltpu.VMEM((N_HEADS, TQ), F32)],
        compiler_params=_params(("parallel", "arbitrary")),
        name="prompt_attn",
    )(rel_bias, qb, qib, kw, kw, kb, vt, dbias)


def _sample_gather_kernel(pt_ref, relb_ref, ptv_ref, q_ref, qi_ref, kw_ref, kn_ref, vn_ref,
                          cki_hbm, ck_hbm, cv_hbm,
                          o_ref, ibuf, kg, vg, keys_s, idx_v, rows_v, idx_sm, acc_s, isem, ksem, vsem, ssem,
                          *, t, past_len, topk, page_base):
    b = pl.program_id(0)
    n_idx_chunks = past_len // IDX_CHUNK
    n_pages = past_len // PAGE
    n_tiles = n_pages + 1
    sel_pages = topk // PAGE
    scale = HEAD_DIM ** -0.5
    nt_dims = (((1,), (1,)), ((), ()))

    def idx_copy(c, p):
        page = page_base + pt_ref[b, c * IDX_PAGES + p]
        return pltpu.make_async_copy(cki_hbm.at[page], ibuf.at[c * IDX_PAGES + p], isem.at[c])

    for c in range(n_idx_chunks):
        for p in range(IDX_PAGES):
            idx_copy(c, p).start()

    qi = qi_ref[0].astype(F32)
    qi_rows = jnp.concatenate([qi[:, h * IDX_DIM:(h + 1) * IDX_DIM] for h in range(N_IDX_HEADS)],
                              axis=0).astype(BF16)
    w = kw_ref[0][:, IDX_DIM:IDX_DIM + N_IDX_HEADS] * (N_IDX_HEADS ** -0.5) * (IDX_DIM ** -0.5)
    w_h = [w[:, h:h + 1] for h in range(N_IDX_HEADS)]

    def scores(ki_bf, transposed):
        if transposed:
            s = jnp.dot(qi_rows, ki_bf, preferred_element_type=F32)
        else:
            s = lax.dot_general(qi_rows, ki_bf, nt_dims, preferred_element_type=F32)
        score = jnp.maximum(s[0:t], 0.0) * w_h[0]
        for h in range(1, N_IDX_HEADS):
            score = score + jnp.maximum(s[h * t:(h + 1) * t], 0.0) * w_h[h]
        return score

    for c in range(n_idx_chunks):
        for p in range(IDX_PAGES):
            idx_copy(c, p).wait()
        ki_t = jnp.concatenate([ibuf[c * IDX_PAGES + p] for p in range(IDX_PAGES)], axis=1)
        key = _sortable(scores(ki_t.astype(BF16), True))
        for k in range(IDX_PAGES):
            keys_s[c * IDX_PAGES + k] = key[:, k * PAGE:(k + 1) * PAGE]

    ki_new = jnp.concatenate([kw_ref[0][:, 0:IDX_DIM], jnp.zeros((PAGE - t, IDX_DIM), F32)], axis=0)
    qrow = lax.broadcasted_iota(I32, (t, PAGE), 0)
    lane = lax.broadcasted_iota(I32, (t, PAGE), 1)
    keys_s[n_tiles - 1] = jnp.where(lane <= qrow, _sortable(scores(ki_new.astype(BF16), False)), INT_MIN)

    pos = (lax.broadcasted_iota(I32, (n_tiles, t, PAGE), 0) * PAGE
           + lax.broadcasted_iota(I32, (n_tiles, t, PAGE), 2))

    def count(pred):
        m = jnp.where(pred, 1, 0).astype(I32)
        part = m[0:n_pages]
        for group in (4, 4):
            part = jnp.sum(part.reshape(group, part.shape[0] // group, t, PAGE), axis=0)
        per_lane = jnp.sum(part, axis=0) + m[n_pages]
        return jnp.sum(per_lane, axis=1, keepdims=True)

    thr = _bisect(lambda x: count(keys_s[...] >= x[None]) >= topk, 32, INT_MIN, (t, 1))
    kall = keys_s[...]
    need = topk - count(kall > thr[None])
    thr1 = jnp.maximum(thr, INT_MIN + 1)[None]

    def tie_bound():
        return _bisect(lambda x: count((keys_s[...] == thr[None]) & (pos < x[None])) < need, 31, 0, (t, 1))

    has_tie = jnp.max(count(kall == thr1) - need) > 0
    xb = lax.cond(has_tie, tie_bound, lambda: jnp.full((t, 1), INT_MAX, I32))
    tied = jnp.where(pos <= xb[None], 1.0, 0.0)
    keys_s[...] = pltpu.bitcast(
        jnp.where(kall > thr1, 1.0, jnp.where(kall == thr1, tied, 0.0)).astype(F32), I32)

    sub_i = lax.broadcasted_iota(I32, (PAGE, PAGE), 0)
    lane_i = lax.broadcasted_iota(I32, (PAGE, PAGE), 1)
    upper = jnp.where(sub_i <= lane_i, 1.0, 0.0).astype(BF16)
    lower = jnp.where(lane_i <= sub_i, 1.0, 0.0).astype(BF16)
    ones = jnp.ones((PAGE, PAGE), BF16)
    slot = lax.broadcasted_iota(I32, (topk, PAGE), 0).astype(F32)
    tile_of_lane = lax.broadcasted_iota(I32, (topk, PAGE), 1).astype(F32)
    valid_rows = []
    for qq in range(t):
        sel = pltpu.bitcast(keys_s[0:n_pages, qq, :], F32).astype(BF16)
        within = jnp.dot(sel, upper, preferred_element_type=F32)
        cnt = jnp.broadcast_to(within[:, PAGE - 1:PAGE], (n_pages, PAGE)).astype(BF16)
        running = jnp.dot(lower, cnt, preferred_element_type=F32)
        running_t = running.T
        before = jnp.where(jnp.concatenate([running_t] * (topk // PAGE), axis=0) <= slot, 1.0, 0.0).astype(BF16)
        tile_r = jnp.dot(before, ones, preferred_element_type=F32)
        base_r = jnp.dot(before, cnt, preferred_element_type=F32)
        onehot = jnp.where(tile_r == tile_of_lane, 1.0, 0.0).astype(BF16)
        within_r = jnp.dot(onehot, within.astype(BF16), preferred_element_type=F32)
        lane_r = jnp.dot(jnp.where(within_r <= slot - base_r, 1.0, 0.0).astype(BF16), ones,
                         preferred_element_type=F32)
        valid = slot < running[n_pages - 1:n_pages, :]
        pos_r = jnp.where(valid, tile_r * PAGE + lane_r, 0.0)
        idx_v[qq:qq + 1, :] = pos_r.T[0:1, :].astype(I32)
        valid_rows.append(jnp.where(valid, 1.0, 0.0).T[0:1, :])
    valid_g = jnp.concatenate(valid_rows, axis=0)

    pos_g = idx_v[...]
    pt_row = jnp.broadcast_to(ptv_ref[0], (t, PAGE))
    tile_g = lax.shift_right_logical(pos_g, 7)
    page_g = jnp.concatenate([jnp.take_along_axis(pt_row, tile_g[:, s * PAGE:(s + 1) * PAGE], axis=1)
                              for s in range(sel_pages)], axis=1)
    rows_v[...] = (page_g + page_base) * PAGE_ROWS + (pos_g & (PAGE - 1)) * N_HEADS
    to_smem = pltpu.make_async_copy(rows_v, idx_sm, ssem)
    to_smem.start()
    to_smem.wait()

    def issue(j, _):
        for qq in range(t):
            src = pl.ds(pl.multiple_of(idx_sm[qq, j], N_HEADS), N_HEADS)
            dst = pl.ds(pl.multiple_of((qq * topk + j) * N_HEADS, N_HEADS), N_HEADS)
            pltpu.make_async_copy(ck_hbm.at[src], kg.at[dst], ksem).start()
            pltpu.make_async_copy(cv_hbm.at[src], vg.at[dst], vsem).start(priority=1)
        return 0

    lax.fori_loop(0, topk, issue, 0)

    n_new = LANES
    n_g = topk * N_HEADS

    def expand(x):
        per_group = LANES // N_HEADS
        lane_slot = lax.broadcasted_iota(I32, (t, LANES), 1) >> 3
        out = []
        for c in range((n_g + n_new) // LANES):
            src = x[:, (c * per_group // LANES) * LANES:(c * per_group // LANES + 1) * LANES]
            out.append(jnp.take_along_axis(src, (c * per_group) % LANES + lane_slot, axis=1))
        return jnp.concatenate(out, axis=1)

    qpos = past_len + lax.broadcasted_iota(I32, (t, topk + PAGE), 0)
    new_lane = lax.broadcasted_iota(I32, (t, PAGE), 1)
    key_pos = jnp.concatenate([pos_g, past_len + new_lane], axis=1)
    sel_new = pltpu.bitcast(keys_s[n_tiles - 1], F32)
    take = jnp.concatenate([valid_g, sel_new], axis=1)
    bucket_x = expand(_t5_bucket(qpos - key_pos).astype(F32))[:, 0:n_g + n_new]
    take_x = expand(take)[:, 0:n_g + n_new]
    head = lax.broadcasted_iota(I32, (t, n_g + n_new), 1) & (N_HEADS - 1)
    in_range = lax.broadcasted_iota(I32, (t, n_g + n_new), 1) < n_g + t * N_HEADS
    bias = jnp.zeros((t, n_g + n_new), F32)
    for h in range(N_HEADS):
        bias = jnp.where(head == h, _bias_of_bucket(bucket_x, relb_ref, h), bias)
    amask = jnp.where((take_x > 0.5) & in_range, 0.0, NEG_BIG)

    qf = q_ref[0].astype(F32)
    pad = jnp.zeros((LANES - t * N_HEADS, HEAD_DIM), F32)
    k_new = jnp.concatenate([kn_ref[0], pad], axis=0).astype(BF16)
    v_new = jnp.concatenate([vn_ref[0], pad], axis=0).astype(BF16)

    def own_head(n):
        return (lax.broadcasted_iota(I32, (N_HEADS, n), 0)
                == (lax.broadcasted_iota(I32, (N_HEADS, n), 1) & (N_HEADS - 1)))

    def gathered(buf, qq):
        return [buf[(qq * sel_pages + s) * PAGE_ROWS:(qq * sel_pages + s + 1) * PAGE_ROWS, :].astype(BF16)
                for s in range(sel_pages)]

    n_rows = t * topk * N_HEADS
    pltpu.make_async_copy(ck_hbm.at[pl.ds(0, n_rows)], kg, ksem).wait()
    lg_rows = []
    for qq in range(t):
        q_heads = jnp.concatenate([qf[qq:qq + 1, h * HEAD_DIM:(h + 1) * HEAD_DIM] for h in range(N_HEADS)],
                                  axis=0).astype(BF16)
        parts = []
        for x_bf in gathered(kg, qq) + [k_new]:
            r = lax.dot_general(q_heads, x_bf, nt_dims, preferred_element_type=F32)
            parts.append(jnp.sum(jnp.where(own_head(r.shape[1]), r, 0.0), axis=0, keepdims=True))
        lg_rows.append(jnp.concatenate(parts, axis=1))
    lg = jnp.concatenate(lg_rows, axis=0) * scale + bias + amask

    def per_head_allreduce(x, op):
        for shift in (8, 16, 32, 64):
            x = op(x, pltpu.roll(x, shift, axis=1))
        return x

    groups = (n_g + n_new) // LANES
    m_run = lg[:, 0:LANES]
    for s in range(1, groups):
        m_run = jnp.maximum(m_run, lg[:, s * LANES:(s + 1) * LANES])
    m_fin = per_head_allreduce(m_run, jnp.maximum)
    pr = jnp.exp(lg - jnp.concatenate([m_fin] * groups, axis=1))
    l_run = pr[:, 0:LANES]
    for s in range(1, groups):
        l_run = l_run + pr[:, s * LANES:(s + 1) * LANES]
    l_fin = per_head_allreduce(l_run, jnp.add)

    pltpu.make_async_copy(cv_hbm.at[pl.ds(0, n_rows)], vg, vsem).wait()
    for qq in range(t):
        acc = jnp.zeros((N_HEADS, HEAD_DIM), F32)
        lane0 = 0
        for x_bf in gathered(vg, qq) + [v_new]:
            n = x_bf.shape[0]
            p_heads = jnp.where(own_head(n), jnp.broadcast_to(pr[qq:qq + 1, lane0:lane0 + n], (N_HEADS, n)), 0.0)
            acc = acc + jnp.dot(p_heads.astype(BF16), x_bf, preferred_element_type=F32)
            lane0 += n
        acc_s[qq * N_HEADS:(qq + 1) * N_HEADS, :] = acc
    for h in range(N_HEADS):
        o_ref[0, :, h * HEAD_DIM:(h + 1) * HEAD_DIM] = (
            acc_s[pl.ds(h, t, stride=N_HEADS), :] / l_fin[:, h:h + 1]).astype(BF16)


def _sample_gather_attn(page_table, rel_bias, qb, qib, kw, k_new, v_new, cki, ck, cv, topk, page_base):
    b, t, _ = qb.shape
    n_pages = page_table.shape[1]
    past_len = n_pages * PAGE
    nqi = N_IDX_HEADS * IDX_DIM
    assert t == SUBLANES and n_pages == PAGE and topk % PAGE == 0 and n_pages % IDX_PAGES == 0
    per_b = lambda bi, pt: (bi, 0, 0)
    grid_spec = pltpu.PrefetchScalarGridSpec(
        num_scalar_prefetch=1,
        grid=(b,),
        in_specs=[pl.BlockSpec(memory_space=pltpu.SMEM),
                  pl.BlockSpec((1, 1, n_pages), per_b),
                  pl.BlockSpec((1, t, D_MODEL), per_b),
                  pl.BlockSpec((1, t, nqi), per_b),
                  pl.BlockSpec((1, t, LANES), per_b),
                  pl.BlockSpec((1, t * N_HEADS, HEAD_DIM), per_b),
                  pl.BlockSpec((1, t * N_HEADS, HEAD_DIM), per_b),
                  pl.BlockSpec(memory_space=pl.ANY),
                  pl.BlockSpec(memory_space=pl.ANY),
                  pl.BlockSpec(memory_space=pl.ANY)],
        out_specs=pl.BlockSpec((1, t, D_MODEL), per_b),
        scratch_shapes=[pltpu.VMEM((n_pages, IDX_DIM, PAGE), F32),
                        pltpu.VMEM((t * topk * N_HEADS, HEAD_DIM), F32),
                        pltpu.VMEM((t * topk * N_HEADS, HEAD_DIM), F32),
                        pltpu.VMEM((n_pages + 1, t, PAGE), I32),
                        pltpu.VMEM((t, topk), I32),
                        pltpu.VMEM((t, topk), I32),
                        pltpu.SMEM((t, topk), I32),
                        pltpu.VMEM((t * N_HEADS, HEAD_DIM), F32),
                        pltpu.SemaphoreType.DMA((n_pages // IDX_PAGES,)),
                        pltpu.SemaphoreType.DMA(()),
                        pltpu.SemaphoreType.DMA(()),
                        pltpu.SemaphoreType.DMA(())])
    return pl.pallas_call(
        functools.partial(_sample_gather_kernel, t=t, past_len=past_len, topk=topk, page_base=page_base),
        grid_spec=grid_spec,
        out_shape=jax.ShapeDtypeStruct((b, t, D_MODEL), BF16),
        compiler_params=_params(("arbitrary",)),
        name="sample_attn",
    )(page_table, rel_bias, page_table.reshape(b, 1, n_pages), qb, qib, kw, k_new, v_new, cki, ck, cv)


def kernel(x_prompt, x_sample, state_conv_a, state_conv_b, cache_k, cache_v, cache_kidx, page_table,
           norm_conv, w_in_conv, conv_a_w, conv_b_w, conv_b_bias, ln_b_gain, ln_b_bias, w_out_conv,
           norm_attn, w_in_attn, w_out_attn, rel_bias, norm_mlp, w_up, w_down, norm_final):
    depth = norm_mlp.shape[0]
    n_pool = cache_k.shape[1]
    past_len = page_table.shape[1] * PAGE
    nqkv = 3 * D_MODEL
    nqi = N_IDX_HEADS * IDX_DIM
    row = lambda a: a.reshape(1, -1)

    cki_flat = jnp.swapaxes(cache_kidx, 2, 3).reshape(-1, IDX_DIM, PAGE)
    ck_flat = cache_k.reshape(-1, HEAD_DIM)
    cv_flat = cache_v.reshape(-1, HEAD_DIM)
    dbias = _prompt_bias(rel_bias)

    def run(x, sample):
        b, t, _ = x.shape
        m = b * t
        tm = min(512, m)
        x2d = x.reshape(m, D_MODEL)
        new_a, new_b, new_k, new_v, new_ki = [], [], [], [], []
        for layer in range(depth):
            i = layer // 2
            last = layer == depth - 1
            if layer % 2 == 0:
                if sample:
                    buf_a, buf_b = state_conv_a[i], state_conv_b[i]
                else:
                    buf_a = jnp.zeros((b, CONV_A_WIDTH - 1, D_A), F32)
                    buf_b = jnp.zeros((b, CONV_B_WIDTH - 1, D_B), F32)
                sa = jnp.pad(buf_a, ((0, 0), (A_HALO - (CONV_A_WIDTH - 1), 0), (0, 0)))
                sb = jnp.pad(buf_b, ((0, 0), (B_HALO - (CONV_B_WIDTH - 1), 0), (0, 0)))
                gb, cx, u = _conv_in(x2d, row(norm_conv[i]), w_in_conv[i].astype(BF16), tm)
                tt = min(512, t)
                y, na, nb = _conv(gb.reshape(b, t, D_A), cx.reshape(b, t, D_A), u.reshape(b, t, D_B), sa, sb,
                                  conv_a_w[i], conv_b_w[i], row(conv_b_bias[i]), row(ln_b_gain[i]),
                                  row(ln_b_bias[i]), tt, min(64, tt))
                new_a.append(na)
                new_b.append(nb)
                mix = y.reshape(m, D_MODEL)
                w_out = w_out_conv[i]
            else:
                w_in = w_in_attn[i]
                wkw = jnp.pad(w_in[:, nqkv + nqi:], ((0, 0), (0, LANES - IDX_DIM - N_IDX_HEADS)))
                wvt = w_in[:, 2 * D_MODEL:nqkv].T.astype(BF16)
                outs = _attn_in(x2d, row(norm_attn[i]), w_in[:, :nqkv].astype(BF16), wvt,
                                w_in[:, nqkv:nqkv + nqi].astype(BF16), wkw.astype(BF16), tm, t, not sample)
                qb, k, v, qib, kw = outs[:5]
                shp = lambda a: a.reshape(b, t, a.shape[-1])
                if sample:
                    topk = min(TOPK_MAX, (past_len + t) // 4)
                    o = _sample_gather_attn(page_table, rel_bias, shp(qb), shp(qib), shp(kw),
                                            k.reshape(b, t * N_HEADS, HEAD_DIM),
                                            v.reshape(b, t * N_HEADS, HEAD_DIM),
                                            cki_flat, ck_flat, cv_flat, topk, i * n_pool)
                else:
                    topk = min(TOPK_MAX, t // 4)
                    kb, vt = outs[5:]
                    o = _prompt_attn(rel_bias, shp(qb), shp(qib), shp(kw), shp(kb), vt, dbias, topk)
                new_k.append(k.reshape(b, t, N_HEADS, HEAD_DIM))
                new_v.append(v.reshape(b, t, N_HEADS, HEAD_DIM))
                new_ki.append(kw[:, :IDX_DIM].reshape(b, t, IDX_DIM))
                mix = o.reshape(m, D_MODEL)
                w_out = w_out_attn[i]
            x2d = _out_mlp(mix, x2d, w_out.astype(BF16), row(norm_mlp[layer]), w_up[layer].astype(BF16),
                           w_down[layer].astype(BF16), row(norm_final), tm, last)
        return (x2d.reshape(b, t, D_MODEL), jnp.stack(new_a), jnp.stack(new_b),
                jnp.stack(new_k), jnp.stack(new_v), jnp.stack(new_ki))

    y_p, ca_p, cb_p, k_p, v_p, ki_p = run(x_prompt, False)
    y_s, ca_s, cb_s, k_s, v_s, ki_s = run(x_sample, True)
    return (y_p, y_s, ca_p, cb_p, k_p, v_p, ki_p, ca_s, cb_s, k_s, v_s, ki_s)
```

```python
import functools
import math

import jax
import jax.numpy as jnp
from jax import lax
from jax.experimental import pallas as pl
from jax.experimental.pallas import tpu as pltpu

F32 = jnp.float32
BF16 = jnp.bfloat16
I32 = jnp.int32

D_MODEL = 1024
D_A = 512
D_B = 512
CONV_A_WIDTH = 3
CONV_B_WIDTH = 31
N_HEADS = 8
HEAD_DIM = 128
N_IDX_HEADS = 8
IDX_DIM = 64
TOPK_MAX = 256
NUM_BUCKETS = 32
MAX_DISTANCE = 128
D_FF = 4096
EPS = 1e-6
PAGE = 128
LANES = 128
SUBLANES = 8

INT_MIN = -(2 ** 31)
INT_MAX = 2 ** 31 - 1
NEG_BIG = -1e30

VMEM_LIMIT_BYTES = 56 * 1024 * 1024

A_HALO = 8
B_HALO = 32

TQ = 256
TK = 128
NEAR_TILES = TQ // TK + 1
def _first_far_distance():
    exact = NUM_BUCKETS // 2
    d = exact
    while exact + int(math.log(d / exact) / math.log(MAX_DISTANCE / exact) * (NUM_BUCKETS - exact)) < NUM_BUCKETS - 1:
        d += 1
    return d


FAR_DISTANCE = _first_far_distance()
LOG2E = math.log2(math.e)
QK_SCALE = (HEAD_DIM ** -0.5) * LOG2E
IDX_PAGES = 16
IDX_CHUNK = IDX_PAGES * PAGE
PAGE_ROWS = PAGE * N_HEADS


def _params(sem):
    return pltpu.CompilerParams(dimension_semantics=sem, vmem_limit_bytes=VMEM_LIMIT_BYTES)


def _const_spec(shape):
    nd = len(shape)
    return pl.BlockSpec(shape, lambda *_: (0,) * nd, pipeline_mode=pl.Buffered(1))


def _rms_bf16(x, g):
    ms = jnp.mean(x * x, axis=-1, keepdims=True)
    return ((x * lax.rsqrt(ms + EPS)) * g).astype(BF16)


def _sigmoid(x):
    return 1.0 / (1.0 + jnp.exp(-x))


def _conv_in_kernel(x_ref, g_ref, w_ref, gb_ref, cx_ref, u_ref):
    xn = _rms_bf16(x_ref[...], g_ref[...])
    proj = jnp.dot(xn, w_ref[...], preferred_element_type=F32)
    gb_ref[...] = proj[:, :D_A]
    cx_ref[...] = proj[:, D_A:2 * D_A] * proj[:, 2 * D_A:3 * D_A]
    u_ref[...] = proj[:, 3 * D_A:3 * D_A + D_B] * _sigmoid(proj[:, 3 * D_A + D_B:])


def _conv_in(x2d, g, w_bf, tm):
    m = x2d.shape[0]
    n = w_bf.shape[1]
    out = jax.ShapeDtypeStruct((m, D_A), F32)
    return pl.pallas_call(
        _conv_in_kernel,
        grid=(m // tm,),
        in_specs=[pl.BlockSpec((tm, D_MODEL), lambda i: (i, 0)),
                  _const_spec((1, D_MODEL)),
                  _const_spec((D_MODEL, n))],
        out_specs=[pl.BlockSpec((tm, D_A), lambda i: (i, 0))] * 3,
        out_shape=[out, out, out],
        compiler_params=_params(("parallel",)),
        name="conv_in",
    )(x2d, g, w_bf)


def _conv_kernel(gb_ref, cx_ref, u_ref, sa_ref, sb_ref, wa_ref, wb_ref, bias_ref, lng_ref, lnb_ref,
                 y_ref, na_ref, nb_ref, xa_s, xb_s, sh_s, *, tt, rc):
    t = pl.program_id(1)

    @pl.when(t == 0)
    def _():
        xa_s[0:A_HALO, :] = sa_ref[0]
        xb_s[0:B_HALO, :] = sb_ref[0]

    @pl.when(t > 0)
    def _():
        xa_s[0:A_HALO, :] = xa_s[tt:tt + A_HALO, :]
        xb_s[0:B_HALO, :] = xb_s[tt:tt + B_HALO, :]

    xa_s[A_HALO:A_HALO + tt, :] = cx_ref[0]
    xb_s[B_HALO:B_HALO + tt, :] = u_ref[0]

    bias = bias_ref[...]
    lng = lng_ref[...]
    lnb = lnb_ref[...]
    a0 = A_HALO - (CONV_A_WIDTH - 1)
    b0 = B_HALO - (CONV_B_WIDTH - 1)
    for c in range(tt // rc):
        r0 = c * rc
        ya = wa_ref[0:1, :] * xa_s[a0 + r0:a0 + r0 + rc, :]
        for j in range(1, CONV_A_WIDTH):
            ya = ya + wa_ref[j:j + 1, :] * xa_s[a0 + j + r0:a0 + j + r0 + rc, :]
        y_ref[0, r0:r0 + rc, 0:D_A] = gb_ref[0, r0:r0 + rc, :] * ya

        yb = None
        for shift in range(SUBLANES):
            offs = [o for o in range(b0, b0 + CONV_B_WIDTH) if o % SUBLANES == shift]
            if not offs:
                continue
            rows_needed = offs[-1] - shift + rc
            sh_s[shift, 0:rows_needed, :] = xb_s[r0 + shift:r0 + shift + rows_needed, :]
            for o in offs:
                term = wb_ref[o - b0:o - b0 + 1, :] * sh_s[shift, o - shift:o - shift + rc, :]
                yb = term if yb is None else yb + term
        yb = yb + bias
        mu = jnp.mean(yb, axis=-1, keepdims=True)
        var = jnp.mean(jnp.square(yb - mu), axis=-1, keepdims=True)
        z = (yb - mu) * lax.rsqrt(var + EPS) * lng + lnb
        y_ref[0, r0:r0 + rc, D_A:D_A + D_B] = z * _sigmoid(z)

    na_ref[0] = xa_s[tt + A_HALO - (CONV_A_WIDTH - 1):tt + A_HALO, :]
    nb_ref[0] = xb_s[tt + B_HALO - (CONV_B_WIDTH - 1):tt + B_HALO, :]


def _conv(gb, cx, u, sa, sb, wa, wb, bias, lng, lnb, tt, rc):
    b, t, _ = cx.shape
    row = lambda i, j: (i, j, 0)
    per_b = lambda i, j: (i, 0, 0)
    return pl.pallas_call(
        functools.partial(_conv_kernel, tt=tt, rc=rc),
        grid=(b, t // tt),
        in_specs=[pl.BlockSpec((1, tt, D_A), row),
                  pl.BlockSpec((1, tt, D_A), row),
                  pl.BlockSpec((1, tt, D_B), row),
                  pl.BlockSpec((1, A_HALO, D_A), per_b),
                  pl.BlockSpec((1, B_HALO, D_B), per_b),
                  pl.BlockSpec((CONV_A_WIDTH, D_A), lambda i, j: (0, 0)),
                  pl.BlockSpec((CONV_B_WIDTH, D_B), lambda i, j: (0, 0)),
                  pl.BlockSpec((1, D_B), lambda i, j: (0, 0)),
                  pl.BlockSpec((1, D_B), lambda i, j: (0, 0)),
                  pl.BlockSpec((1, D_B), lambda i, j: (0, 0))],
        out_specs=[pl.BlockSpec((1, tt, D_MODEL), row),
                   pl.BlockSpec((1, CONV_A_WIDTH - 1, D_A), per_b),
                   pl.BlockSpec((1, CONV_B_WIDTH - 1, D_B), per_b)],
        out_shape=[jax.ShapeDtypeStruct((b, t, D_MODEL), F32),
                   jax.ShapeDtypeStruct((b, CONV_A_WIDTH - 1, D_A), F32),
                   jax.ShapeDtypeStruct((b, CONV_B_WIDTH - 1, D_B), F32)],
        scratch_shapes=[pltpu.VMEM((A_HALO + tt, D_A), F32),
                        pltpu.VMEM((B_HALO + tt, D_B), F32),
                        pltpu.VMEM((SUBLANES, B_HALO + rc, D_B), F32)],
        compiler_params=_params(("parallel", "arbitrary")),
        name="conv_mix",
    )(gb, cx, u, sa, sb, wa, wb, bias, lng, lnb)


def _out_mlp_kernel(a_ref, res_ref, wo_ref, g_ref, wu_ref, wd_ref, gf_ref, o_ref, *, final_norm, ff_chunk):
    x1 = res_ref[...] + jnp.dot(a_ref[...].astype(BF16), wo_ref[...], preferred_element_type=F32)
    xn = _rms_bf16(x1, g_ref[...])
    acc = x1
    for c in range(D_FF // ff_chunk):
        h = jnp.dot(xn, wu_ref[:, c * ff_chunk:(c + 1) * ff_chunk], preferred_element_type=F32)
        h = jnp.square(jnp.maximum(h, 0.0)).astype(BF16)
        acc = acc + jnp.dot(h, wd_ref[c * ff_chunk:(c + 1) * ff_chunk, :], preferred_element_type=F32)
    if final_norm:
        ms = jnp.mean(acc * acc, axis=-1, keepdims=True)
        acc = (acc * lax.rsqrt(ms + EPS)) * gf_ref[...]
    o_ref[...] = acc


def _out_mlp(a2d, res2d, wo_bf, g, wu_bf, wd_bf, gf, tm, final_norm):
    m = res2d.shape[0]
    row = lambda i: (i, 0)
    return pl.pallas_call(
        functools.partial(_out_mlp_kernel, final_norm=final_norm, ff_chunk=1024),
        grid=(m // tm,),
        in_specs=[pl.BlockSpec((tm, D_MODEL), row),
                  pl.BlockSpec((tm, D_MODEL), row),
                  _const_spec((D_MODEL, D_MODEL)),
                  _const_spec((1, D_MODEL)),
                  _const_spec((D_MODEL, D_FF)),
                  _const_spec((D_FF, D_MODEL)),
                  _const_spec((1, D_MODEL))],
        out_specs=pl.BlockSpec((tm, D_MODEL), row),
        out_shape=jax.ShapeDtypeStruct((m, D_MODEL), F32),
        compiler_params=_params(("parallel",)),
        name="out_mlp",
    )(a2d, res2d, wo_bf, g, wu_bf, wd_bf, gf)


def _attn_in_kernel(*refs, prompt, tm):
    if prompt:
        (x_ref, g_ref, wqkv_ref, wvt_ref, wqi_ref, wkw_ref,
         qb_ref, k_ref, v_ref, qib_ref, kw_ref, kb_ref, vt_ref) = refs
    else:
        x_ref, g_ref, wqkv_ref, wqi_ref, wkw_ref, qb_ref, k_ref, v_ref, qib_ref, kw_ref = refs
    xn = _rms_bf16(x_ref[...], g_ref[...])
    q = jnp.dot(xn, wqkv_ref[:, 0:D_MODEL], preferred_element_type=F32)
    qb_ref[...] = (q * QK_SCALE if prompt else q).astype(BF16)
    k = jnp.dot(xn, wqkv_ref[:, D_MODEL:2 * D_MODEL], preferred_element_type=F32)
    v = jnp.dot(xn, wqkv_ref[:, 2 * D_MODEL:3 * D_MODEL], preferred_element_type=F32)
    for h in range(N_HEADS):
        k_ref[:, h, :] = k[:, h * HEAD_DIM:(h + 1) * HEAD_DIM]
        v_ref[:, h, :] = v[:, h * HEAD_DIM:(h + 1) * HEAD_DIM]
    qib_ref[...] = jnp.dot(xn, wqi_ref[...], preferred_element_type=F32).astype(BF16)
    kw_ref[...] = jnp.dot(xn, wkw_ref[...], preferred_element_type=F32)
    if prompt:
        kb_ref[...] = k.astype(BF16)
        vt = lax.dot_general(wvt_ref[...], xn, (((1,), (1,)), ((), ())),
                             preferred_element_type=F32).astype(BF16)
        for s in range(tm // TK):
            vt_ref[0, s] = vt[:, s * TK:(s + 1) * TK]


def _attn_in(x2d, g, wqkv_bf, wvt_bf, wqi_bf, wkw_bf, tm, seq_len, prompt):
    m = x2d.shape[0]
    row = lambda i: (i, 0)
    row3 = lambda i: (i, 0, 0)
    nqi = N_IDX_HEADS * IDX_DIM
    in_specs = [pl.BlockSpec((tm, D_MODEL), row),
                _const_spec((1, D_MODEL)),
                _const_spec((D_MODEL, 3 * D_MODEL))]
    args = [x2d, g, wqkv_bf]
    if prompt:
        in_specs.append(_const_spec((D_MODEL, D_MODEL)))
        args.append(wvt_bf)
    in_specs += [_const_spec((D_MODEL, nqi)), _const_spec((D_MODEL, LANES))]
    args += [wqi_bf, wkw_bf]
    out_specs = [pl.BlockSpec((tm, D_MODEL), row),
                 pl.BlockSpec((tm, N_HEADS, HEAD_DIM), row3),
                 pl.BlockSpec((tm, N_HEADS, HEAD_DIM), row3),
                 pl.BlockSpec((tm, nqi), row),
                 pl.BlockSpec((tm, LANES), row)]
    out_shape = [jax.ShapeDtypeStruct((m, D_MODEL), BF16),
                 jax.ShapeDtypeStruct((m, N_HEADS, HEAD_DIM), F32),
                 jax.ShapeDtypeStruct((m, N_HEADS, HEAD_DIM), F32),
                 jax.ShapeDtypeStruct((m, nqi), BF16),
                 jax.ShapeDtypeStruct((m, LANES), F32)]
    if prompt:
        steps_per_seq = seq_len // tm
        out_specs += [pl.BlockSpec((tm, D_MODEL), row),
                      pl.BlockSpec((1, tm // TK, D_MODEL, TK),
                                   lambda i: (i // steps_per_seq, i % steps_per_seq, 0, 0))]
        out_shape += [jax.ShapeDtypeStruct((m, D_MODEL), BF16),
                      jax.ShapeDtypeStruct((m // seq_len, seq_len // TK, D_MODEL, TK), BF16)]
    return pl.pallas_call(
        functools.partial(_attn_in_kernel, prompt=prompt, tm=tm),
        grid=(m // tm,),
        in_specs=in_specs,
        out_specs=out_specs,
        out_shape=out_shape,
        compiler_params=_params(("parallel",)),
        name="attn_in",
    )(*args)


def _t5_bucket(d):
    n = jnp.maximum(d, 0)
    max_exact = NUM_BUCKETS // 2
    nf = jnp.maximum(n, 1).astype(F32)
    large = max_exact + (jnp.log(nf / max_exact) / math.log(MAX_DISTANCE / max_exact)
                         * (NUM_BUCKETS - max_exact)).astype(I32)
    large = jnp.minimum(large, NUM_BUCKETS - 1)
    return jnp.where(n < max_exact, n, large)


def _bias_of_bucket(bucket, relb_ref, h):
    val = jnp.full(bucket.shape, relb_ref[NUM_BUCKETS - 1, h], F32)
    for bk in range(NUM_BUCKETS - 2, -1, -1):
        val = jnp.where(bucket == bk, relb_ref[bk, h], val)
    return val


def _prompt_bias_kernel(relb_ref, o_ref):
    h = pl.program_id(0)
    c = lax.broadcasted_iota(I32, (NEAR_TILES * TK, TQ), 0)
    r = lax.broadcasted_iota(I32, (NEAR_TILES * TK, TQ), 1)
    o_ref[0] = _bias_of_bucket(_t5_bucket(r + TK - c), relb_ref, h) * LOG2E


def _prompt_bias(rel_bias):
    return pl.pallas_call(
        _prompt_bias_kernel,
        grid=(N_HEADS,),
        in_specs=[pl.BlockSpec(memory_space=pltpu.SMEM)],
        out_specs=pl.BlockSpec((1, NEAR_TILES * TK, TQ), lambda h: (h, 0, 0)),
        out_shape=jax.ShapeDtypeStruct((N_HEADS, NEAR_TILES * TK, TQ), F32),
        compiler_params=_params(("parallel",)),
        name="prompt_bias",
    )(rel_bias)


def _sortable(score):
    bits = pltpu.bitcast(score, I32)
    return jnp.where(bits < 0, bits ^ INT_MAX, bits)


def _bisect(count_fn, n_bits, init, shape):
    def body(it, cur):
        trial = cur ^ lax.shift_left(jnp.int32(1), jnp.asarray(n_bits - 1 - it, I32))
        return jnp.where(count_fn(trial), trial, cur)
    return lax.fori_loop(0, n_bits, body, jnp.full(shape, init, I32))


def _prompt_attn_kernel(relb_ref, q_ref, qi_ref, kwq_ref, kw_ref, k_ref, vt_ref, db_ref, o_ref,
                        keys_s, acc_s, m_s, l_s, *, topk):
    i = pl.program_id(1)
    tiles_per_q = TQ // TK
    nt = (i + 1) * tiles_per_q
    nt_dims = (((1,), (1,)), ((), ()))
    row = (1, TQ)

    qi = qi_ref[0]
    qi_h = [qi[:, h * IDX_DIM:(h + 1) * IDX_DIM] for h in range(N_IDX_HEADS)]
    kwq_t = kwq_ref[0].T
    w_scale = (N_IDX_HEADS ** -0.5) * (IDX_DIM ** -0.5)
    w_h = [kwq_t[IDX_DIM + h:IDX_DIM + h + 1, :] * w_scale for h in range(N_IDX_HEADS)]
    krow = lax.broadcasted_iota(I32, (TK, TQ), 0)
    qpos = i * TQ + lax.broadcasted_iota(I32, (TK, TQ), 1)

    def score_chunk(jj, _):
        r0 = pl.multiple_of(jj * TQ, TQ)
        ki_t = kw_ref[0, pl.ds(r0, TQ), 0:IDX_DIM].astype(BF16)
        score = jnp.zeros((TQ, TQ), F32)
        for h in range(N_IDX_HEADS):
            s = lax.dot_general(ki_t, qi_h[h], nt_dims, preferred_element_type=F32)
            score = score + jnp.maximum(s, 0.0) * w_h[h]
        key = _sortable(score)
        for s in range(tiles_per_q):
            j = jj * tiles_per_q + s
            keys_s[j] = jnp.where(j * TK + krow <= qpos, key[s * TK:(s + 1) * TK, :], INT_MIN)
        return 0

    lax.fori_loop(0, i + 1, score_chunk, 0)

    def count(pred_fn):
        def tile_count(j):
            m = jnp.where(pred_fn(keys_s[j], j), 1, 0).astype(I32)
            return jnp.sum(m.reshape(TK // SUBLANES, SUBLANES, TQ), axis=0)

        def body(jj, c):
            for s in range(tiles_per_q):
                c = c + tile_count(jj * tiles_per_q + s)
            return c
        c = lax.fori_loop(0, i + 1, body, jnp.zeros((SUBLANES, TQ), I32))
        return jnp.sum(c, axis=0, keepdims=True)

    thr = _bisect(lambda x: count(lambda kt, j: kt >= x) >= topk, 32, INT_MIN, row)
    n_gt = count(lambda kt, j: kt > thr)
    thr1 = jnp.maximum(thr, INT_MIN + 1)
    n_eq = count(lambda kt, j: kt == thr1)
    need = topk - n_gt

    def tie_bound():
        return _bisect(lambda x: count(lambda kt, j: jnp.where(kt == thr, j * TK + krow, INT_MAX) < x) < need,
                       31, 0, row)

    has_tie = jnp.max(n_eq - need) > 0
    xb = lax.cond(has_tie, tie_bound, lambda: jnp.full(row, INT_MAX, I32))

    q = q_ref[0]
    q_h = [q[:, h * HEAD_DIM:(h + 1) * HEAD_DIM] for h in range(N_HEADS)]
    acc_s[...] = jnp.zeros_like(acc_s)
    m_s[...] = jnp.full_like(m_s, NEG_BIG)
    l_s[...] = jnp.zeros_like(l_s)

    def attend(j, near):
        r0 = pl.multiple_of(j * TK, TK)
        kt = keys_s[j]
        tied = jnp.where(j * TK + krow <= xb, 0.0, -jnp.inf)
        amask = jnp.where(kt > thr1, 0.0, jnp.where(kt == thr1, tied, -jnp.inf))
        for h in range(N_HEADS):
            hs = slice(h * HEAD_DIM, (h + 1) * HEAD_DIM)
            k_t = k_ref[0, pl.ds(r0, TK), hs]
            s = lax.dot_general(k_t, q_h[h], nt_dims, preferred_element_type=F32)
            m_old = m_s[h:h + 1, :]
            if near is None:
                c = relb_ref[NUM_BUCKETS - 1, h] * LOG2E
                lg = s + amask
                m_new = jnp.maximum(m_old, jnp.max(lg, axis=0, keepdims=True) + c)
                p = jnp.exp2(lg - (m_new - c))
            else:
                lg = s + (amask + db_ref[h, near * TK:(near + 1) * TK, :])
                m_new = jnp.maximum(m_old, jnp.max(lg, axis=0, keepdims=True))
                p = jnp.exp2(lg - m_new)
            alpha = jnp.exp2(m_old - m_new)
            l_s[h:h + 1, :] = alpha * l_s[h:h + 1, :] + jnp.sum(p, axis=0, keepdims=True)
            m_s[h:h + 1, :] = m_new
            pv = jnp.dot(vt_ref[0, j, hs, :], p.astype(BF16), preferred_element_type=F32)
            acc_s[hs, :] = alpha * acc_s[hs, :] + pv

    first_near = i * tiles_per_q - 1

    def far_pair(jj, _):
        attend(2 * jj, None)
        attend(2 * jj + 1, None)
        return 0

    lax.fori_loop(0, lax.shift_right_logical(jnp.maximum(first_near, 0), 1), far_pair, 0)

    @pl.when(jnp.logical_and(i >= 1, (first_near & 1) == 1))
    def _():
        attend(first_near - 1, None)

    @pl.when(i >= 1)
    def _():
        attend(first_near, 0)

    for near in range(1, NEAR_TILES):
        attend(first_near + near, near)

    for h in range(N_HEADS):
        hs = slice(h * HEAD_DIM, (h + 1) * HEAD_DIM)
        o_ref[0, :, hs] = (acc_s[hs, :] / l_s[h:h + 1, :]).T.astype(BF16)


def _prompt_attn(rel_bias, qb, qib, kw, kb, vt, dbias, topk):
    b, t, _ = qb.shape
    nqi = N_IDX_HEADS * IDX_DIM
    assert TK + 1 >= FAR_DISTANCE and TQ % TK == 0
    return pl.pallas_call(
        functools.partial(_prompt_attn_kernel, topk=topk),
        grid=(b, t // TQ),
        in_specs=[pl.BlockSpec(memory_space=pltpu.SMEM),
                  pl.BlockSpec((1, TQ, D_MODEL), lambda bi, i: (bi, i, 0)),
                  pl.BlockSpec((1, TQ, nqi), lambda bi, i: (bi, i, 0)),
                  pl.BlockSpec((1, TQ, LANES), lambda bi, i: (bi, i, 0)),
                  pl.BlockSpec((1, t, LANES), lambda bi, i: (bi, 0, 0)),
                  pl.BlockSpec((1, t, D_MODEL), lambda bi, i: (bi, 0, 0)),
                  pl.BlockSpec((1, t // TK, D_MODEL, TK), lambda bi, i: (bi, 0, 0, 0)),
                  _const_spec((N_HEADS, NEAR_TILES * TK, TQ))],
        out_specs=pl.BlockSpec((1, TQ, D_MODEL), lambda bi, i: (bi, i, 0)),
        out_shape=jax.ShapeDtypeStruct((b, t, D_MODEL), BF16),
        scratch_shapes=[pltpu.VMEM((t // TK, TK, TQ), I32),
                        pltpu.VMEM((D_MODEL, TQ), F32),
                        pltpu.VMEM((N_HEADS, TQ), F32),
                        pltpu.VMEM((N_HEADS, TQ), F32)],
        compiler_params=_params(("parallel", "arbitrary")),
        name="prompt_attn",
    )(rel_bias, qb, qib, kw, kw, kb, vt, dbias)


def _sample_gather_kernel(pt_ref, relb_ref, ptv_ref, q_ref, qi_ref, kw_ref, kn_ref, vn_ref,
                          cki_hbm, ck_hbm, cv_hbm,
                          o_ref, ibuf, kg, vg, keys_s, idx_v, rows_v, idx_sm, acc_s, isem, ksem, vsem, ssem,
                          *, t, past_len, topk, page_base):
    b = pl.program_id(0)
    n_idx_chunks = past_len // IDX_CHUNK
    n_pages = past_len // PAGE
    n_tiles = n_pages + 1
    sel_pages = topk // PAGE
    scale = HEAD_DIM ** -0.5
    nt_dims = (((1,), (1,)), ((), ()))

    def idx_copy(c, p):
        page = page_base + pt_ref[b, c * IDX_PAGES + p]
        return pltpu.make_async_copy(cki_hbm.at[page], ibuf.at[c * IDX_PAGES + p], isem.at[c])

    for c in range(n_idx_chunks):
        for p in range(IDX_PAGES):
            idx_copy(c, p).start()

    qi = qi_ref[0].astype(F32)
    qi_rows = jnp.concatenate([qi[:, h * IDX_DIM:(h + 1) * IDX_DIM] for h in range(N_IDX_HEADS)],
                              axis=0).astype(BF16)
    w = kw_ref[0][:, IDX_DIM:IDX_DIM + N_IDX_HEADS] * (N_IDX_HEADS ** -0.5) * (IDX_DIM ** -0.5)
    w_h = [w[:, h:h + 1] for h in range(N_IDX_HEADS)]

    def scores(ki_bf, transposed):
        if transposed:
            s = jnp.dot(qi_rows, ki_bf, preferred_element_type=F32)
        else:
            s = lax.dot_general(qi_rows, ki_bf, nt_dims, preferred_element_type=F32)
        score = jnp.maximum(s[0:t], 0.0) * w_h[0]
        for h in range(1, N_IDX_HEADS):
            score = score + jnp.maximum(s[h * t:(h + 1) * t], 0.0) * w_h[h]
        return score

    for c in range(n_idx_chunks):
        for p in range(IDX_PAGES):
            idx_copy(c, p).wait()
        ki_t = jnp.concatenate([ibuf[c * IDX_PAGES + p] for p in range(IDX_PAGES)], axis=1)
        key = _sortable(scores(ki_t.astype(BF16), True))
        for k in range(IDX_PAGES):
            keys_s[c * IDX_PAGES + k] = key[:, k * PAGE:(k + 1) * PAGE]

    ki_new = jnp.concatenate([kw_ref[0][:, 0:IDX_DIM], jnp.zeros((PAGE - t, IDX_DIM), F32)], axis=0)
    qrow = lax.broadcasted_iota(I32, (t, PAGE), 0)
    lane = lax.broadcasted_iota(I32, (t, PAGE), 1)
    keys_s[n_tiles - 1] = jnp.where(lane <= qrow, _sortable(scores(ki_new.astype(BF16), False)), INT_MIN)

    pos = (lax.broadcasted_iota(I32, (n_tiles, t, PAGE), 0) * PAGE
           + lax.broadcasted_iota(I32, (n_tiles, t, PAGE), 2))

    def count(pred):
        m = jnp.where(pred, 1, 0).astype(I32)
        part = m[0:n_pages]
        for group in (4, 4):
            part = jnp.sum(part.reshape(group, part.shape[0] // group, t, PAGE), axis=0)
        per_lane = jnp.sum(part, axis=0) + m[n_pages]
        return jnp.sum(per_lane, axis=1, keepdims=True)

    thr = _bisect(lambda x: count(keys_s[...] >= x[None]) >= topk, 32, INT_MIN, (t, 1))
    kall = keys_s[...]
    need = topk - count(kall > thr[None])
    thr1 = jnp.maximum(thr, INT_MIN + 1)[None]

    def tie_bound():
        return _bisect(lambda x: count((keys_s[...] == thr[None]) & (pos < x[None])) < need, 31, 0, (t, 1))

    has_tie = jnp.max(count(kall == thr1) - need) > 0
    xb = lax.cond(has_tie, tie_bound, lambda: jnp.full((t, 1), INT_MAX, I32))
    tied = jnp.where(pos <= xb[None], 1.0, 0.0)
    keys_s[...] = pltpu.bitcast(
        jnp.where(kall > thr1, 1.0, jnp.where(kall == thr1, tied, 0.0)).astype(F32), I32)

    sub_i = lax.broadcasted_iota(I32, (PAGE, PAGE), 0)
    lane_i = lax.broadcasted_iota(I32, (PAGE, PAGE), 1)
    upper = jnp.where(sub_i <= lane_i, 1.0, 0.0).astype(BF16)
    lower = jnp.where(lane_i <= sub_i, 1.0, 0.0).astype(BF16)
    ones = jnp.ones((PAGE, PAGE), BF16)
    slot = lax.broadcasted_iota(I32, (topk, PAGE), 0).astype(F32)
    tile_of_lane = lax.broadcasted_iota(I32, (topk, PAGE), 1).astype(F32)
    queries = range(t)
    mm = functools.partial(jnp.dot, preferred_element_type=F32)
    sel = [pltpu.bitcast(keys_s[0:n_pages, qq, :], F32).astype(BF16) for qq in queries]
    within = [mm(s, upper) for s in sel]
    cnt = [jnp.broadcast_to(w_[:, PAGE - 1:PAGE], (n_pages, PAGE)).astype(BF16) for w_ in within]
    running = [mm(lower, c) for c in cnt]
    before = [jnp.where(jnp.concatenate([r.T] * (topk // PAGE), axis=0) <= slot, 1.0, 0.0).astype(BF16)
              for r in running]
    tile_r = [mm(bf, ones) for bf in before]
    base_r = [mm(bf, c) for bf, c in zip(before, cnt)]
    onehot = [jnp.where(tr == tile_of_lane, 1.0, 0.0).astype(BF16) for tr in tile_r]
    within_r = [mm(oh, w_.astype(BF16)) for oh, w_ in zip(onehot, within)]
    lane_r = [mm(jnp.where(wr <= slot - br, 1.0, 0.0).astype(BF16), ones) for wr, br in zip(within_r, base_r)]
    valid = [slot < r[n_pages - 1:n_pages, :] for r in running]
    for qq in queries:
        pos_r = jnp.where(valid[qq], tile_r[qq] * PAGE + lane_r[qq], 0.0)
        idx_v[qq:qq + 1, :] = pos_r.T[0:1, :].astype(I32)
    valid_g = jnp.concatenate([jnp.where(v, 1.0, 0.0).T[0:1, :] for v in valid], axis=0)

    pos_g = idx_v[...]
    pt_row = jnp.broadcast_to(ptv_ref[0], (t, PAGE))
    tile_g = lax.shift_right_logical(pos_g, 7)
    page_g = jnp.concatenate([jnp.take_along_axis(pt_row, tile_g[:, s * PAGE:(s + 1) * PAGE], axis=1)
                              for s in range(sel_pages)], axis=1)
    rows_v[...] = (page_g + page_base) * PAGE_ROWS + (pos_g & (PAGE - 1)) * N_HEADS
    to_smem = pltpu.make_async_copy(rows_v, idx_sm, ssem)
    to_smem.start()
    to_smem.wait()

    def issue(j, _):
        for qq in range(t):
            src = pl.ds(pl.multiple_of(idx_sm[qq, j], N_HEADS), N_HEADS)
            dst = pl.ds(pl.multiple_of((qq * topk + j) * N_HEADS, N_HEADS), N_HEADS)
            pltpu.make_async_copy(ck_hbm.at[src], kg.at[dst], ksem).start()
            pltpu.make_async_copy(cv_hbm.at[src], vg.at[dst], vsem).start(priority=1)
        return 0

    lax.fori_loop(0, topk, issue, 0)

    n_new = LANES
    n_g = topk * N_HEADS

    def expand(x):
        per_group = LANES // N_HEADS
        lane_slot = lax.broadcasted_iota(I32, (t, LANES), 1) >> 3
        out = []
        for c in range((n_g + n_new) // LANES):
            src = x[:, (c * per_group // LANES) * LANES:(c * per_group // LANES + 1) * LANES]
            out.append(jnp.take_along_axis(src, (c * per_group) % LANES + lane_slot, axis=1))
        return jnp.concatenate(out, axis=1)

    qpos = past_len + lax.broadcasted_iota(I32, (t, topk + PAGE), 0)
    new_lane = lax.broadcasted_iota(I32, (t, PAGE), 1)
    key_pos = jnp.concatenate([pos_g, past_len + new_lane], axis=1)
    sel_new = pltpu.bitcast(keys_s[n_tiles - 1], F32)
    take = jnp.concatenate([valid_g, sel_new], axis=1)
    bucket_x = expand(_t5_bucket(qpos - key_pos).astype(F32))[:, 0:n_g + n_new]
    take_x = expand(take)[:, 0:n_g + n_new]
    head = lax.broadcasted_iota(I32, (t, n_g + n_new), 1) & (N_HEADS - 1)
    in_range = lax.broadcasted_iota(I32, (t, n_g + n_new), 1) < n_g + t * N_HEADS
    bias = jnp.zeros((t, n_g + n_new), F32)
    for h in range(N_HEADS):
        bias = jnp.where(head == h, _bias_of_bucket(bucket_x, relb_ref, h), bias)
    amask = jnp.where((take_x > 0.5) & in_range, 0.0, NEG_BIG)

    qf = q_ref[0].astype(F32)
    pad = jnp.zeros((LANES - t * N_HEADS, HEAD_DIM), F32)
    k_new = jnp.concatenate([kn_ref[0], pad], axis=0).astype(BF16)
    v_new = jnp.concatenate([vn_ref[0], pad], axis=0).astype(BF16)

    def own_head(n):
        return (lax.broadcasted_iota(I32, (N_HEADS, n), 0)
                == (lax.broadcasted_iota(I32, (N_HEADS, n), 1) & (N_HEADS - 1)))

    def gathered(buf, qq):
        return [buf[(qq * sel_pages + s) * PAGE_ROWS:(qq * sel_pages + s + 1) * PAGE_ROWS, :].astype(BF16)
                for s in range(sel_pages)]

    n_rows = t * topk * N_HEADS
    pltpu.make_async_copy(ck_hbm.at[pl.ds(0, n_rows)], kg, ksem).wait()
    lg_rows = []
    for qq in range(t):
        q_heads = jnp.concatenate([qf[qq:qq + 1, h * HEAD_DIM:(h + 1) * HEAD_DIM] for h in range(N_HEADS)],
                                  axis=0).astype(BF16)
        parts = []
        for x_bf in gathered(kg, qq) + [k_new]:
            r = lax.dot_general(q_heads, x_bf, nt_dims, preferred_element_type=F32)
            parts.append(jnp.sum(jnp.where(own_head(r.shape[1]), r, 0.0), axis=0, keepdims=True))
        lg_rows.append(jnp.concatenate(parts, axis=1))
    lg = jnp.concatenate(lg_rows, axis=0) * scale + bias + amask

    def per_head_allreduce(x, op):
        for shift in (8, 16, 32, 64):
            x = op(x, pltpu.roll(x, shift, axis=1))
        return x

    groups = (n_g + n_new) // LANES
    m_run = lg[:, 0:LANES]
    for s in range(1, groups):
        m_run = jnp.maximum(m_run, lg[:, s * LANES:(s + 1) * LANES])
    m_fin = per_head_allreduce(m_run, jnp.maximum)
    pr = jnp.exp(lg - jnp.concatenate([m_fin] * groups, axis=1))
    l_run = pr[:, 0:LANES]
    for s in range(1, groups):
        l_run = l_run + pr[:, s * LANES:(s + 1) * LANES]
    l_fin = per_head_allreduce(l_run, jnp.add)

    pltpu.make_async_copy(cv_hbm.at[pl.ds(0, n_rows)], vg, vsem).wait()
    for qq in range(t):
        acc = jnp.zeros((N_HEADS, HEAD_DIM), F32)
        lane0 = 0
        for x_bf in gathered(vg, qq) + [v_new]:
            n = x_bf.shape[0]
            p_heads = jnp.where(own_head(n), jnp.broadcast_to(pr[qq:qq + 1, lane0:lane0 + n], (N_HEADS, n)), 0.0)
            acc = acc + jnp.dot(p_heads.astype(BF16), x_bf, preferred_element_type=F32)
            lane0 += n
        acc_s[qq * N_HEADS:(qq + 1) * N_HEADS, :] = acc
    for h in range(N_HEADS):
        o_ref[0, :, h * HEAD_DIM:(h + 1) * HEAD_DIM] = (
            acc_s[pl.ds(h, t, stride=N_HEADS), :] / l_fin[:, h:h + 1]).astype(BF16)


def _sample_gather_attn(page_table, rel_bias, qb, qib, kw, k_new, v_new, cki, ck, cv, topk, page_base):
    b, t, _ = qb.shape
    n_pages = page_table.shape[1]
    past_len = n_pages * PAGE
    nqi = N_IDX_HEADS * IDX_DIM
    assert t == SUBLANES and n_pages == PAGE and topk % PAGE == 0 and n_pages % IDX_PAGES == 0
    per_b = lambda bi, pt: (bi, 0, 0)
    grid_spec = pltpu.PrefetchScalarGridSpec(
        num_scalar_prefetch=1,
        grid=(b,),
        in_specs=[pl.BlockSpec(memory_space=pltpu.SMEM),
                  pl.BlockSpec((1, 1, n_pages), per_b),
                  pl.BlockSpec((1, t, D_MODEL), per_b),
                  pl.BlockSpec((1, t, nqi), per_b),
                  pl.BlockSpec((1, t, LANES), per_b),
                  pl.BlockSpec((1, t * N_HEADS, HEAD_DIM), per_b),
                  pl.BlockSpec((1, t * N_HEADS, HEAD_DIM), per_b),
                  pl.BlockSpec(memory_space=pl.ANY),
                  pl.BlockSpec(memory_space=pl.ANY),
                  pl.BlockSpec(memory_space=pl.ANY)],
        out_specs=pl.BlockSpec((1, t, D_MODEL), per_b),
        scratch_shapes=[pltpu.VMEM((n_pages, IDX_DIM, PAGE), F32),
                        pltpu.VMEM((t * topk * N_HEADS, HEAD_DIM), F32),
                        pltpu.VMEM((t * topk * N_HEADS, HEAD_DIM), F32),
                        pltpu.VMEM((n_pages + 1, t, PAGE), I32),
                        pltpu.VMEM((t, topk), I32),
                        pltpu.VMEM((t, topk), I32),
                        pltpu.SMEM((t, topk), I32),
                        pltpu.VMEM((t * N_HEADS, HEAD_DIM), F32),
                        pltpu.SemaphoreType.DMA((n_pages // IDX_PAGES,)),
                        pltpu.SemaphoreType.DMA(()),
                        pltpu.SemaphoreType.DMA(()),
                        pltpu.SemaphoreType.DMA(())])
    return pl.pallas_call(
        functools.partial(_sample_gather_kernel, t=t, past_len=past_len, topk=topk, page_base=page_base),
        grid_spec=grid_spec,
        out_shape=jax.ShapeDtypeStruct((b, t, D_MODEL), BF16),
        compiler_params=_params(("arbitrary",)),
        name="sample_attn",
    )(page_table, rel_bias, page_table.reshape(b, 1, n_pages), qb, qib, kw, k_new, v_new, cki, ck, cv)


def kernel(x_prompt, x_sample, state_conv_a, state_conv_b, cache_k, cache_v, cache_kidx, page_table,
           norm_conv, w_in_conv, conv_a_w, conv_b_w, conv_b_bias, ln_b_gain, ln_b_bias, w_out_conv,
           norm_attn, w_in_attn, w_out_attn, rel_bias, norm_mlp, w_up, w_down, norm_final):
    depth = norm_mlp.shape[0]
    n_pool = cache_k.shape[1]
    past_len = page_table.shape[1] * PAGE
    nqkv = 3 * D_MODEL
    nqi = N_IDX_HEADS * IDX_DIM
    row = lambda a: a.reshape(1, -1)

    cki_flat = jnp.swapaxes(cache_kidx, 2, 3).reshape(-1, IDX_DIM, PAGE)
    ck_flat = cache_k.reshape(-1, HEAD_DIM)
    cv_flat = cache_v.reshape(-1, HEAD_DIM)
    dbias = _prompt_bias(rel_bias)

    def run(x, sample):
        b, t, _ = x.shape
        m = b * t
        tm = min(512, m)
        x2d = x.reshape(m, D_MODEL)
        new_a, new_b, new_k, new_v, new_ki = [], [], [], [], []
        for layer in range(depth):
            i = layer // 2
            last = layer == depth - 1
            if layer % 2 == 0:
                if sample:
                    buf_a, buf_b = state_conv_a[i], state_conv_b[i]
                else:
                    buf_a = jnp.zeros((b, CONV_A_WIDTH - 1, D_A), F32)
                    buf_b = jnp.zeros((b, CONV_B_WIDTH - 1, D_B), F32)
                sa = jnp.pad(buf_a, ((0, 0), (A_HALO - (CONV_A_WIDTH - 1), 0), (0, 0)))
                sb = jnp.pad(buf_b, ((0, 0), (B_HALO - (CONV_B_WIDTH - 1), 0), (0, 0)))
                gb, cx, u = _conv_in(x2d, row(norm_conv[i]), w_in_conv[i].astype(BF16), tm)
                tt = min(512, t)
                y, na, nb = _conv(gb.reshape(b, t, D_A), cx.reshape(b, t, D_A), u.reshape(b, t, D_B), sa, sb,
                                  conv_a_w[i], conv_b_w[i], row(conv_b_bias[i]), row(ln_b_gain[i]),
                                  row(ln_b_bias[i]), tt, min(64, tt))
                new_a.append(na)
                new_b.append(nb)
                mix = y.reshape(m, D_MODEL)
                w_out = w_out_conv[i]
            else:
                w_in = w_in_attn[i]
                wkw = jnp.pad(w_in[:, nqkv + nqi:], ((0, 0), (0, LANES - IDX_DIM - N_IDX_HEADS)))
                wvt = w_in[:, 2 * D_MODEL:nqkv].T.astype(BF16)
                outs = _attn_in(x2d, row(norm_attn[i]), w_in[:, :nqkv].astype(BF16), wvt,
                                w_in[:, nqkv:nqkv + nqi].astype(BF16), wkw.astype(BF16), tm, t, not sample)
                qb, k, v, qib, kw = outs[:5]
                shp = lambda a: a.reshape(b, t, a.shape[-1])
                if sample:
                    topk = min(TOPK_MAX, (past_len + t) // 4)
                    o = _sample_gather_attn(page_table, rel_bias, shp(qb), shp(qib), shp(kw),
                                            k.reshape(b, t * N_HEADS, HEAD_DIM),
                                            v.reshape(b, t * N_HEADS, HEAD_DIM),
                                            cki_flat, ck_flat, cv_flat, topk, i * n_pool)
                else:
                    topk = min(TOPK_MAX, t // 4)
                    kb, vt = outs[5:]
                    o = _prompt_attn(rel_bias, shp(qb), shp(qib), shp(kw), shp(kb), vt, dbias, topk)
                new_k.append(k.reshape(b, t, N_HEADS, HEAD_DIM))
                new_v.append(v.reshape(b, t, N_HEADS, HEAD_DIM))
                new_ki.append(kw[:, :IDX_DIM].reshape(b, t, IDX_DIM))
                mix = o.reshape(m, D_MODEL)
                w_out = w_out_attn[i]
            x2d = _out_mlp(mix, x2d, w_out.astype(BF16), row(norm_mlp[layer]), w_up[layer].astype(BF16),
                           w_down[layer].astype(BF16), row(norm_final), tm, last)
        return (x2d.reshape(b, t, D_MODEL), jnp.stack(new_a), jnp.stack(new_b),
                jnp.stack(new_k), jnp.stack(new_v), jnp.stack(new_ki))

    y_p, ca_p, cb_p, k_p, v_p, ki_p = run(x_prompt, False)
    y_s, ca_s, cb_s, k_s, v_s, ki_s = run(x_sample, True)
    return (y_p, y_s, ca_p, cb_p, k_p, v_p, ki_p, ca_s, cb_s, k_s, v_s, ki_s)
```

```python
import functools
import math

import jax
import jax.numpy as jnp
from jax import lax
from jax.experimental import pallas as pl
from jax.experimental.pallas import tpu as pltpu

F32 = jnp.float32
BF16 = jnp.bfloat16
I32 = jnp.int32

D_MODEL = 1024
D_A = 512
D_B = 512
CONV_A_WIDTH = 3
CONV_B_WIDTH = 31
N_HEADS = 8
HEAD_DIM = 128
N_IDX_HEADS = 8
IDX_DIM = 64
TOPK_MAX = 256
NUM_BUCKETS = 32
MAX_DISTANCE = 128
D_FF = 4096
EPS = 1e-6
PAGE = 128
LANES = 128
SUBLANES = 8

INT_MIN = -(2 ** 31)
INT_MAX = 2 ** 31 - 1
NEG_BIG = -1e30

VMEM_LIMIT_BYTES = 56 * 1024 * 1024

A_HALO = 8
B_HALO = 32

TQ = 256
TK = 128
NEAR_TILES = TQ // TK + 1
def _first_far_distance():
    exact = NUM_BUCKETS // 2
    d = exact
    while exact + int(math.log(d / exact) / math.log(MAX_DISTANCE / exact) * (NUM_BUCKETS - exact)) < NUM_BUCKETS - 1:
        d += 1
    return d


FAR_DISTANCE = _first_far_distance()
LOG2E = math.log2(math.e)
QK_SCALE = (HEAD_DIM ** -0.5) * LOG2E
IDX_PAGES = 16
IDX_CHUNK = IDX_PAGES * PAGE
PAGE_ROWS = PAGE * N_HEADS


def _params(sem):
    return pltpu.CompilerParams(dimension_semantics=sem, vmem_limit_bytes=VMEM_LIMIT_BYTES)


def _const_spec(shape):
    nd = len(shape)
    return pl.BlockSpec(shape, lambda *_: (0,) * nd, pipeline_mode=pl.Buffered(1))


def _rms_bf16(x, g):
    ms = jnp.mean(x * x, axis=-1, keepdims=True)
    return ((x * lax.rsqrt(ms + EPS)) * g).astype(BF16)


def _sigmoid(x):
    return 1.0 / (1.0 + jnp.exp(-x))


def _conv_in_kernel(x_ref, g_ref, w_ref, gb_ref, cx_ref, u_ref):
    xn = _rms_bf16(x_ref[...], g_ref[...])
    proj = jnp.dot(xn, w_ref[...], preferred_element_type=F32)
    gb_ref[...] = proj[:, :D_A]
    cx_ref[...] = proj[:, D_A:2 * D_A] * proj[:, 2 * D_A:3 * D_A]
    u_ref[...] = proj[:, 3 * D_A:3 * D_A + D_B] * _sigmoid(proj[:, 3 * D_A + D_B:])


def _conv_in(x2d, g, w_bf, tm):
    m = x2d.shape[0]
    n = w_bf.shape[1]
    out = jax.ShapeDtypeStruct((m, D_A), F32)
    return pl.pallas_call(
        _conv_in_kernel,
        grid=(m // tm,),
        in_specs=[pl.BlockSpec((tm, D_MODEL), lambda i: (i, 0)),
                  _const_spec((1, D_MODEL)),
                  _const_spec((D_MODEL, n))],
        out_specs=[pl.BlockSpec((tm, D_A), lambda i: (i, 0))] * 3,
        out_shape=[out, out, out],
        compiler_params=_params(("parallel",)),
        name="conv_in",
    )(x2d, g, w_bf)


def _conv_kernel(gb_ref, cx_ref, u_ref, sa_ref, sb_ref, wa_ref, wb_ref, bias_ref, lng_ref, lnb_ref,
                 y_ref, na_ref, nb_ref, xa_s, xb_s, sh_s, *, tt, rc):
    t = pl.program_id(1)

    @pl.when(t == 0)
    def _():
        xa_s[0:A_HALO, :] = sa_ref[0]
        xb_s[0:B_HALO, :] = sb_ref[0]

    @pl.when(t > 0)
    def _():
        xa_s[0:A_HALO, :] = xa_s[tt:tt + A_HALO, :]
        xb_s[0:B_HALO, :] = xb_s[tt:tt + B_HALO, :]

    xa_s[A_HALO:A_HALO + tt, :] = cx_ref[0]
    xb_s[B_HALO:B_HALO + tt, :] = u_ref[0]

    bias = bias_ref[...]
    lng = lng_ref[...]
    lnb = lnb_ref[...]
    a0 = A_HALO - (CONV_A_WIDTH - 1)
    b0 = B_HALO - (CONV_B_WIDTH - 1)
    for c in range(tt // rc):
        r0 = c * rc
        ya = wa_ref[0:1, :] * xa_s[a0 + r0:a0 + r0 + rc, :]
        for j in range(1, CONV_A_WIDTH):
            ya = ya + wa_ref[j:j + 1, :] * xa_s[a0 + j + r0:a0 + j + r0 + rc, :]
        y_ref[0, r0:r0 + rc, 0:D_A] = gb_ref[0, r0:r0 + rc, :] * ya

        yb = None
        for shift in range(SUBLANES):
            offs = [o for o in range(b0, b0 + CONV_B_WIDTH) if o % SUBLANES == shift]
            if not offs:
                continue
            rows_needed = offs[-1] - shift + rc
            sh_s[shift, 0:rows_needed, :] = xb_s[r0 + shift:r0 + shift + rows_needed, :]
            for o in offs:
                term = wb_ref[o - b0:o - b0 + 1, :] * sh_s[shift, o - shift:o - shift + rc, :]
                yb = term if yb is None else yb + term
        yb = yb + bias
        mu = jnp.mean(yb, axis=-1, keepdims=True)
        var = jnp.mean(jnp.square(yb - mu), axis=-1, keepdims=True)
        z = (yb - mu) * lax.rsqrt(var + EPS) * lng + lnb
        y_ref[0, r0:r0 + rc, D_A:D_A + D_B] = z * _sigmoid(z)

    na_ref[0] = xa_s[tt + A_HALO - (CONV_A_WIDTH - 1):tt + A_HALO, :]
    nb_ref[0] = xb_s[tt + B_HALO - (CONV_B_WIDTH - 1):tt + B_HALO, :]


def _conv(gb, cx, u, sa, sb, wa, wb, bias, lng, lnb, tt, rc):
    b, t, _ = cx.shape
    row = lambda i, j: (i, j, 0)
    per_b = lambda i, j: (i, 0, 0)
    return pl.pallas_call(
        functools.partial(_conv_kernel, tt=tt, rc=rc),
        grid=(b, t // tt),
        in_specs=[pl.BlockSpec((1, tt, D_A), row),
                  pl.BlockSpec((1, tt, D_A), row),
                  pl.BlockSpec((1, tt, D_B), row),
                  pl.BlockSpec((1, A_HALO, D_A), per_b),
                  pl.BlockSpec((1, B_HALO, D_B), per_b),
                  pl.BlockSpec((CONV_A_WIDTH, D_A), lambda i, j: (0, 0)),
                  pl.BlockSpec((CONV_B_WIDTH, D_B), lambda i, j: (0, 0)),
                  pl.BlockSpec((1, D_B), lambda i, j: (0, 0)),
                  pl.BlockSpec((1, D_B), lambda i, j: (0, 0)),
                  pl.BlockSpec((1, D_B), lambda i, j: (0, 0))],
        out_specs=[pl.BlockSpec((1, tt, D_MODEL), row),
                   pl.BlockSpec((1, CONV_A_WIDTH - 1, D_A), per_b),
                   pl.BlockSpec((1, CONV_B_WIDTH - 1, D_B), per_b)],
        out_shape=[jax.ShapeDtypeStruct((b, t, D_MODEL), F32),
                   jax.ShapeDtypeStruct((b, CONV_A_WIDTH - 1, D_A), F32),
                   jax.ShapeDtypeStruct((b, CONV_B_WIDTH - 1, D_B), F32)],
        scratch_shapes=[pltpu.VMEM((A_HALO + tt, D_A), F32),
                        pltpu.VMEM((B_HALO + tt, D_B), F32),
                        pltpu.VMEM((SUBLANES, B_HALO + rc, D_B), F32)],
        compiler_params=_params(("parallel", "arbitrary")),
        name="conv_mix",
    )(gb, cx, u, sa, sb, wa, wb, bias, lng, lnb)


def _out_mlp_kernel(a_ref, res_ref, wo_ref, g_ref, wu_ref, wd_ref, gf_ref, o_ref, *, final_norm, ff_chunk):
    x1 = res_ref[...] + jnp.dot(a_ref[...].astype(BF16), wo_ref[...], preferred_element_type=F32)
    xn = _rms_bf16(x1, g_ref[...])
    acc = x1
    for c in range(D_FF // ff_chunk):
        h = jnp.dot(xn, wu_ref[:, c * ff_chunk:(c + 1) * ff_chunk], preferred_element_type=F32)
        h = jnp.square(jnp.maximum(h, 0.0)).astype(BF16)
        acc = acc + jnp.dot(h, wd_ref[c * ff_chunk:(c + 1) * ff_chunk, :], preferred_element_type=F32)
    if final_norm:
        ms = jnp.mean(acc * acc, axis=-1, keepdims=True)
        acc = (acc * lax.rsqrt(ms + EPS)) * gf_ref[...]
    o_ref[...] = acc


def _out_mlp(a2d, res2d, wo_bf, g, wu_bf, wd_bf, gf, tm, final_norm):
    m = res2d.shape[0]
    row = lambda i: (i, 0)
    return pl.pallas_call(
        functools.partial(_out_mlp_kernel, final_norm=final_norm, ff_chunk=1024),
        grid=(m // tm,),
        in_specs=[pl.BlockSpec((tm, D_MODEL), row),
                  pl.BlockSpec((tm, D_MODEL), row),
                  _const_spec((D_MODEL, D_MODEL)),
                  _const_spec((1, D_MODEL)),
                  _const_spec((D_MODEL, D_FF)),
                  _const_spec((D_FF, D_MODEL)),
                  _const_spec((1, D_MODEL))],
        out_specs=pl.BlockSpec((tm, D_MODEL), row),
        out_shape=jax.ShapeDtypeStruct((m, D_MODEL), F32),
        compiler_params=_params(("parallel",)),
        name="out_mlp",
    )(a2d, res2d, wo_bf, g, wu_bf, wd_bf, gf)


def _attn_in_kernel(*refs, prompt, tm):
    if prompt:
        (x_ref, g_ref, wqkv_ref, wvt_ref, wqi_ref, wkw_ref,
         qb_ref, k_ref, v_ref, qib_ref, kw_ref, kb_ref, vt_ref) = refs
    else:
        x_ref, g_ref, wqkv_ref, wqi_ref, wkw_ref, qb_ref, k_ref, v_ref, qib_ref, kw_ref = refs
    xn = _rms_bf16(x_ref[...], g_ref[...])
    q = jnp.dot(xn, wqkv_ref[:, 0:D_MODEL], preferred_element_type=F32)
    qb_ref[...] = (q * QK_SCALE if prompt else q).astype(BF16)
    k = jnp.dot(xn, wqkv_ref[:, D_MODEL:2 * D_MODEL], preferred_element_type=F32)
    v = jnp.dot(xn, wqkv_ref[:, 2 * D_MODEL:3 * D_MODEL], preferred_element_type=F32)
    for h in range(N_HEADS):
        k_ref[:, h, :] = k[:, h * HEAD_DIM:(h + 1) * HEAD_DIM]
        v_ref[:, h, :] = v[:, h * HEAD_DIM:(h + 1) * HEAD_DIM]
    qib_ref[...] = jnp.dot(xn, wqi_ref[...], preferred_element_type=F32).astype(BF16)
    kw_ref[...] = jnp.dot(xn, wkw_ref[...], preferred_element_type=F32)
    if prompt:
        kb_ref[...] = k.astype(BF16)
        vt = lax.dot_general(wvt_ref[...], xn, (((1,), (1,)), ((), ())),
                             preferred_element_type=F32).astype(BF16)
        for s in range(tm // TK):
            vt_ref[0, s] = vt[:, s * TK:(s + 1) * TK]


def _attn_in(x2d, g, wqkv_bf, wvt_bf, wqi_bf, wkw_bf, tm, seq_len, prompt):
    m = x2d.shape[0]
    row = lambda i: (i, 0)
    row3 = lambda i: (i, 0, 0)
    nqi = N_IDX_HEADS * IDX_DIM
    in_specs = [pl.BlockSpec((tm, D_MODEL), row),
                _const_spec((1, D_MODEL)),
                _const_spec((D_MODEL, 3 * D_MODEL))]
    args = [x2d, g, wqkv_bf]
    if prompt:
        in_specs.append(_const_spec((D_MODEL, D_MODEL)))
        args.append(wvt_bf)
    in_specs += [_const_spec((D_MODEL, nqi)), _const_spec((D_MODEL, LANES))]
    args += [wqi_bf, wkw_bf]
    out_specs = [pl.BlockSpec((tm, D_MODEL), row),
                 pl.BlockSpec((tm, N_HEADS, HEAD_DIM), row3),
                 pl.BlockSpec((tm, N_HEADS, HEAD_DIM), row3),
                 pl.BlockSpec((tm, nqi), row),
                 pl.BlockSpec((tm, LANES), row)]
    out_shape = [jax.ShapeDtypeStruct((m, D_MODEL), BF16),
                 jax.ShapeDtypeStruct((m, N_HEADS, HEAD_DIM), F32),
                 jax.ShapeDtypeStruct((m, N_HEADS, HEAD_DIM), F32),
                 jax.ShapeDtypeStruct((m, nqi), BF16),
                 jax.ShapeDtypeStruct((m, LANES), F32)]
    if prompt:
        steps_per_seq = seq_len // tm
        out_specs += [pl.BlockSpec((tm, D_MODEL), row),
                      pl.BlockSpec((1, tm // TK, D_MODEL, TK),
                                   lambda i: (i // steps_per_seq, i % steps_per_seq, 0, 0))]
        out_shape += [jax.ShapeDtypeStruct((m, D_MODEL), BF16),
                      jax.ShapeDtypeStruct((m // seq_len, seq_len // TK, D_MODEL, TK), BF16)]
    return pl.pallas_call(
        functools.partial(_attn_in_kernel, prompt=prompt, tm=tm),
        grid=(m // tm,),
        in_specs=in_specs,
        out_specs=out_specs,
        out_shape=out_shape,
        compiler_params=_params(("parallel",)),
        name="attn_in",
    )(*args)


def _t5_bucket(d):
    n = jnp.maximum(d, 0)
    max_exact = NUM_BUCKETS // 2
    nf = jnp.maximum(n, 1).astype(F32)
    large = max_exact + (jnp.log(nf / max_exact) / math.log(MAX_DISTANCE / max_exact)
                         * (NUM_BUCKETS - max_exact)).astype(I32)
    large = jnp.minimum(large, NUM_BUCKETS - 1)
    return jnp.where(n < max_exact, n, large)


def _bias_of_bucket(bucket, relb_ref, h):
    val = jnp.full(bucket.shape, relb_ref[NUM_BUCKETS - 1, h], F32)
    for bk in range(NUM_BUCKETS - 2, -1, -1):
        val = jnp.where(bucket == bk, relb_ref[bk, h], val)
    return val


def _prompt_bias_kernel(relb_ref, o_ref):
    h = pl.program_id(0)
    c = lax.broadcasted_iota(I32, (NEAR_TILES * TK, TQ), 0)
    r = lax.broadcasted_iota(I32, (NEAR_TILES * TK, TQ), 1)
    o_ref[0] = _bias_of_bucket(_t5_bucket(r + TK - c), relb_ref, h) * LOG2E


def _prompt_bias(rel_bias):
    return pl.pallas_call(
        _prompt_bias_kernel,
        grid=(N_HEADS,),
        in_specs=[pl.BlockSpec(memory_space=pltpu.SMEM)],
        out_specs=pl.BlockSpec((1, NEAR_TILES * TK, TQ), lambda h: (h, 0, 0)),
        out_shape=jax.ShapeDtypeStruct((N_HEADS, NEAR_TILES * TK, TQ), F32),
        compiler_params=_params(("parallel",)),
        name="prompt_bias",
    )(rel_bias)


def _sortable(score):
    bits = pltpu.bitcast(score, I32)
    return jnp.where(bits < 0, bits ^ INT_MAX, bits)


def _bisect(count_fn, n_bits, init, shape):
    def body(it, cur):
        trial = cur ^ lax.shift_left(jnp.int32(1), jnp.asarray(n_bits - 1 - it, I32))
        return jnp.where(count_fn(trial), trial, cur)
    return lax.fori_loop(0, n_bits, body, jnp.full(shape, init, I32))


def _prompt_attn_kernel(relb_ref, q_ref, qi_ref, kwq_ref, kw_ref, k_ref, vt_ref, db_ref, o_ref,
                        keys_s, acc_s, m_s, l_s, *, topk):
    i = pl.program_id(1)
    tiles_per_q = TQ // TK
    nt = (i + 1) * tiles_per_q
    nt_dims = (((1,), (1,)), ((), ()))
    row = (1, TQ)

    qi = qi_ref[0]
    qi_h = [qi[:, h * IDX_DIM:(h + 1) * IDX_DIM] for h in range(N_IDX_HEADS)]
    kwq_t = kwq_ref[0].T
    w_scale = (N_IDX_HEADS ** -0.5) * (IDX_DIM ** -0.5)
    w_h = [kwq_t[IDX_DIM + h:IDX_DIM + h + 1, :] * w_scale for h in range(N_IDX_HEADS)]
    krow = lax.broadcasted_iota(I32, (TK, TQ), 0)
    qpos = i * TQ + lax.broadcasted_iota(I32, (TK, TQ), 1)

    def score_chunk(jj, _):
        r0 = pl.multiple_of(jj * TQ, TQ)
        ki_t = kw_ref[0, pl.ds(r0, TQ), 0:IDX_DIM].astype(BF16)
        score = jnp.zeros((TQ, TQ), F32)
        for h in range(N_IDX_HEADS):
            s = lax.dot_general(ki_t, qi_h[h], nt_dims, preferred_element_type=F32)
            score = score + jnp.maximum(s, 0.0) * w_h[h]
        key = _sortable(score)
        for s in range(tiles_per_q):
            j = jj * tiles_per_q + s
            keys_s[j] = jnp.where(j * TK + krow <= qpos, key[s * TK:(s + 1) * TK, :], INT_MIN)
        return 0

    lax.fori_loop(0, i + 1, score_chunk, 0)

    def count(pred_fn):
        def tile_count(j):
            m = jnp.where(pred_fn(keys_s[j], j), 1, 0).astype(I32)
            return jnp.sum(m.reshape(TK // SUBLANES, SUBLANES, TQ), axis=0)

        def body(jj, c):
            for s in range(tiles_per_q):
                c = c + tile_count(jj * tiles_per_q + s)
            return c
        c = lax.fori_loop(0, i + 1, body, jnp.zeros((SUBLANES, TQ), I32))
        return jnp.sum(c, axis=0, keepdims=True)

    thr = _bisect(lambda x: count(lambda kt, j: kt >= x) >= topk, 32, INT_MIN, row)
    n_gt = count(lambda kt, j: kt > thr)
    thr1 = jnp.maximum(thr, INT_MIN + 1)
    n_eq = count(lambda kt, j: kt == thr1)
    need = topk - n_gt

    def tie_bound():
        return _bisect(lambda x: count(lambda kt, j: jnp.where(kt == thr, j * TK + krow, INT_MAX) < x) < need,
                       31, 0, row)

    has_tie = jnp.max(n_eq - need) > 0
    xb = lax.cond(has_tie, tie_bound, lambda: jnp.full(row, INT_MAX, I32))

    q = q_ref[0]
    q_h = [q[:, h * HEAD_DIM:(h + 1) * HEAD_DIM] for h in range(N_HEADS)]
    acc_s[...] = jnp.zeros_like(acc_s)
    m_s[...] = jnp.full_like(m_s, NEG_BIG)
    l_s[...] = jnp.zeros_like(l_s)

    def attend(j, near):
        r0 = pl.multiple_of(j * TK, TK)
        kt = keys_s[j]
        tied = jnp.where(j * TK + krow <= xb, 0.0, -jnp.inf)
        amask = jnp.where(kt > thr1, 0.0, jnp.where(kt == thr1, tied, -jnp.inf))
        for h in range(N_HEADS):
            hs = slice(h * HEAD_DIM, (h + 1) * HEAD_DIM)
            k_t = k_ref[0, pl.ds(r0, TK), hs]
            s = lax.dot_general(k_t, q_h[h], nt_dims, preferred_element_type=F32)
            m_old = m_s[h:h + 1, :]
            if near is None:
                c = relb_ref[NUM_BUCKETS - 1, h] * LOG2E
                lg = s + amask
                m_new = jnp.maximum(m_old, jnp.max(lg, axis=0, keepdims=True) + c)
                p = jnp.exp2(lg - (m_new - c))
            else:
                lg = s + (amask + db_ref[h, near * TK:(near + 1) * TK, :])
                m_new = jnp.maximum(m_old, jnp.max(lg, axis=0, keepdims=True))
                p = jnp.exp2(lg - m_new)
            alpha = jnp.exp2(m_old - m_new)
            l_s[h:h + 1, :] = alpha * l_s[h:h + 1, :] + jnp.sum(p, axis=0, keepdims=True)
            m_s[h:h + 1, :] = m_new
            pv = jnp.dot(vt_ref[0, j, hs, :], p.astype(BF16), preferred_element_type=F32)
            acc_s[hs, :] = alpha * acc_s[hs, :] + pv

    first_near = i * tiles_per_q - 1

    def far_quad(jj, _):
        for s in range(4):
            attend(4 * jj + s, None)
        return 0

    n_far = jnp.maximum(first_near, 0)
    lax.fori_loop(0, lax.shift_right_logical(n_far, 2), far_quad, 0)
    rest = n_far & ~3

    @pl.when((n_far & 2) == 2)
    def _():
        attend(rest, None)
        attend(rest + 1, None)

    @pl.when((n_far & 1) == 1)
    def _():
        attend(n_far - 1, None)

    @pl.when(i >= 1)
    def _():
        attend(first_near, 0)

    for near in range(1, NEAR_TILES):
        attend(first_near + near, near)

    for h in range(N_HEADS):
        hs = slice(h * HEAD_DIM, (h + 1) * HEAD_DIM)
        o_ref[0, :, hs] = (acc_s[hs, :] / l_s[h:h + 1, :]).T.astype(BF16)


def _prompt_attn(rel_bias, qb, qib, kw, kb, vt, dbias, topk):
    b, t, _ = qb.shape
    nqi = N_IDX_HEADS * IDX_DIM
    assert TK + 1 >= FAR_DISTANCE and TQ % TK == 0
    return pl.pallas_call(
        functools.partial(_prompt_attn_kernel, topk=topk),
        grid=(b, t // TQ),
        in_specs=[pl.BlockSpec(memory_space=pltpu.SMEM),
                  pl.BlockSpec((1, TQ, D_MODEL), lambda bi, i: (bi, i, 0)),
                  pl.BlockSpec((1, TQ, nqi), lambda bi, i: (bi, i, 0)),
                  pl.BlockSpec((1, TQ, LANES), lambda bi, i: (bi, i, 0)),
                  pl.BlockSpec((1, t, LANES), lambda bi, i: (bi, 0, 0)),
                  pl.BlockSpec((1, t, D_MODEL), lambda bi, i: (bi, 0, 0)),
                  pl.BlockSpec((1, t // TK, D_MODEL, TK), lambda bi, i: (bi, 0, 0, 0)),
                  _const_spec((N_HEADS, NEAR_TILES * TK, TQ))],
        out_specs=pl.BlockSpec((1, TQ, D_MODEL), lambda bi, i: (bi, i, 0)),
        out_shape=jax.ShapeDtypeStruct((b, t, D_MODEL), BF16),
        scratch_shapes=[pltpu.VMEM((t // TK, TK, TQ), I32),
                        pltpu.VMEM((D_MODEL, TQ), F32),
                        pltpu.VMEM((N_HEADS, TQ), F32),
                        pltpu.VMEM((N_HEADS, TQ), F32)],
        compiler_params=_params(("parallel", "arbitrary")),
        name="prompt_attn",
    )(rel_bias, qb, qib, kw, kw, kb, vt, dbias)


def _sample_gather_kernel(pt_ref, relb_ref, ptv_ref, q_ref, qi_ref, kw_ref, kn_ref, vn_ref,
                          cki_hbm, ck_hbm, cv_hbm,
                          o_ref, ibuf, kg, vg, keys_s, idx_v, rows_v, idx_sm, acc_s, isem, ksem, vsem, ssem,
                          *, t, past_len, topk, page_base):
    b = pl.program_id(0)
    n_idx_chunks = past_len // IDX_CHUNK
    n_pages = past_len // PAGE
    n_tiles = n_pages + 1
    sel_pages = topk // PAGE
    scale = HEAD_DIM ** -0.5
    nt_dims = (((1,), (1,)), ((), ()))

    def idx_copy(c, p):
        page = page_base + pt_ref[b, c * IDX_PAGES + p]
        return pltpu.make_async_copy(cki_hbm.at[page], ibuf.at[c * IDX_PAGES + p], isem.at[c])

    for c in range(n_idx_chunks):
        for p in range(IDX_PAGES):
            idx_copy(c, p).start()

    qi = qi_ref[0].astype(F32)
    qi_rows = jnp.concatenate([qi[:, h * IDX_DIM:(h + 1) * IDX_DIM] for h in range(N_IDX_HEADS)],
                              axis=0).astype(BF16)
    w = kw_ref[0][:, IDX_DIM:IDX_DIM + N_IDX_HEADS] * (N_IDX_HEADS ** -0.5) * (IDX_DIM ** -0.5)
    w_h = [w[:, h:h + 1] for h in range(N_IDX_HEADS)]

    def scores(ki_bf, transposed):
        if transposed:
            s = jnp.dot(qi_rows, ki_bf, preferred_element_type=F32)
        else:
            s = lax.dot_general(qi_rows, ki_bf, nt_dims, preferred_element_type=F32)
        score = jnp.maximum(s[0:t], 0.0) * w_h[0]
        for h in range(1, N_IDX_HEADS):
            score = score + jnp.maximum(s[h * t:(h + 1) * t], 0.0) * w_h[h]
        return score

    for c in range(n_idx_chunks):
        for p in range(IDX_PAGES):
            idx_copy(c, p).wait()
        ki_t = jnp.concatenate([ibuf[c * IDX_PAGES + p] for p in range(IDX_PAGES)], axis=1)
        key = _sortable(scores(ki_t.astype(BF16), True))
        for k in range(IDX_PAGES):
            keys_s[c * IDX_PAGES + k] = key[:, k * PAGE:(k + 1) * PAGE]

    ki_new = jnp.concatenate([kw_ref[0][:, 0:IDX_DIM], jnp.zeros((PAGE - t, IDX_DIM), F32)], axis=0)
    qrow = lax.broadcasted_iota(I32, (t, PAGE), 0)
    lane = lax.broadcasted_iota(I32, (t, PAGE), 1)
    keys_s[n_tiles - 1] = jnp.where(lane <= qrow, _sortable(scores(ki_new.astype(BF16), False)), INT_MIN)

    pos = (lax.broadcasted_iota(I32, (n_tiles, t, PAGE), 0) * PAGE
           + lax.broadcasted_iota(I32, (n_tiles, t, PAGE), 2))

    def count(pred):
        m = jnp.where(pred, 1, 0).astype(I32)
        part = m[0:n_pages]
        for group in (4, 4):
            part = jnp.sum(part.reshape(group, part.shape[0] // group, t, PAGE), axis=0)
        per_lane = jnp.sum(part, axis=0) + m[n_pages]
        return jnp.sum(per_lane, axis=1, keepdims=True)

    thr = _bisect(lambda x: count(keys_s[...] >= x[None]) >= topk, 32, INT_MIN, (t, 1))
    kall = keys_s[...]
    need = topk - count(kall > thr[None])
    thr1 = jnp.maximum(thr, INT_MIN + 1)[None]

    def tie_bound():
        return _bisect(lambda x: count((keys_s[...] == thr[None]) & (pos < x[None])) < need, 31, 0, (t, 1))

    has_tie = jnp.max(count(kall == thr1) - need) > 0
    xb = lax.cond(has_tie, tie_bound, lambda: jnp.full((t, 1), INT_MAX, I32))
    tied = jnp.where(pos <= xb[None], 1.0, 0.0)
    keys_s[...] = pltpu.bitcast(
        jnp.where(kall > thr1, 1.0, jnp.where(kall == thr1, tied, 0.0)).astype(F32), I32)

    sub_i = lax.broadcasted_iota(I32, (PAGE, PAGE), 0)
    lane_i = lax.broadcasted_iota(I32, (PAGE, PAGE), 1)
    upper = jnp.where(sub_i <= lane_i, 1.0, 0.0).astype(BF16)
    lower = jnp.where(lane_i <= sub_i, 1.0, 0.0).astype(BF16)
    ones = jnp.ones((PAGE, PAGE), BF16)
    slot = lax.broadcasted_iota(I32, (topk, PAGE), 0).astype(F32)
    tile_of_lane = lax.broadcasted_iota(I32, (topk, PAGE), 1).astype(F32)
    queries = range(t)
    mm = functools.partial(jnp.dot, preferred_element_type=F32)
    sel = [pltpu.bitcast(keys_s[0:n_pages, qq, :], F32).astype(BF16) for qq in queries]
    within = [mm(s, upper) for s in sel]
    cnt = [jnp.broadcast_to(w_[:, PAGE - 1:PAGE], (n_pages, PAGE)).astype(BF16) for w_ in within]
    running = [mm(lower, c) for c in cnt]
    before = [jnp.where(jnp.concatenate([r.T] * (topk // PAGE), axis=0) <= slot, 1.0, 0.0).astype(BF16)
              for r in running]
    tile_r = [mm(bf, ones) for bf in before]
    base_r = [mm(bf, c) for bf, c in zip(before, cnt)]
    onehot = [jnp.where(tr == tile_of_lane, 1.0, 0.0).astype(BF16) for tr in tile_r]
    within_r = [mm(oh, w_.astype(BF16)) for oh, w_ in zip(onehot, within)]
    lane_r = [mm(jnp.where(wr <= slot - br, 1.0, 0.0).astype(BF16), ones) for wr, br in zip(within_r, base_r)]
    valid = [slot < r[n_pages - 1:n_pages, :] for r in running]
    for qq in queries:
        pos_r = jnp.where(valid[qq], tile_r[qq] * PAGE + lane_r[qq], 0.0)
        idx_v[qq:qq + 1, :] = pos_r.T[0:1, :].astype(I32)
    valid_g = jnp.concatenate([jnp.where(v, 1.0, 0.0).T[0:1, :] for v in valid], axis=0)

    pos_g = idx_v[...]
    pt_row = jnp.broadcast_to(ptv_ref[0], (t, PAGE))
    tile_g = lax.shift_right_logical(pos_g, 7)
    page_g = jnp.concatenate([jnp.take_along_axis(pt_row, tile_g[:, s * PAGE:(s + 1) * PAGE], axis=1)
                              for s in range(sel_pages)], axis=1)
    rows_v[...] = (page_g + page_base) * PAGE_ROWS + (pos_g & (PAGE - 1)) * N_HEADS
    to_smem = pltpu.make_async_copy(rows_v, idx_sm, ssem)
    to_smem.start()
    to_smem.wait()

    def issue(j, _):
        for qq in range(t):
            src = pl.ds(pl.multiple_of(idx_sm[qq, j], N_HEADS), N_HEADS)
            dst = pl.ds(pl.multiple_of((qq * topk + j) * N_HEADS, N_HEADS), N_HEADS)
            pltpu.make_async_copy(ck_hbm.at[src], kg.at[dst], ksem).start()
            pltpu.make_async_copy(cv_hbm.at[src], vg.at[dst], vsem).start(priority=1)
        return 0

    lax.fori_loop(0, topk, issue, 0)

    n_new = LANES
    n_g = topk * N_HEADS

    def expand(x):
        per_group = LANES // N_HEADS
        lane_slot = lax.broadcasted_iota(I32, (t, LANES), 1) >> 3
        out = []
        for c in range((n_g + n_new) // LANES):
            src = x[:, (c * per_group // LANES) * LANES:(c * per_group // LANES + 1) * LANES]
            out.append(jnp.take_along_axis(src, (c * per_group) % LANES + lane_slot, axis=1))
        return jnp.concatenate(out, axis=1)

    qpos = past_len + lax.broadcasted_iota(I32, (t, topk + PAGE), 0)
    new_lane = lax.broadcasted_iota(I32, (t, PAGE), 1)
    key_pos = jnp.concatenate([pos_g, past_len + new_lane], axis=1)
    sel_new = pltpu.bitcast(keys_s[n_tiles - 1], F32)
    take = jnp.concatenate([valid_g, sel_new], axis=1)
    bucket_x = expand(_t5_bucket(qpos - key_pos).astype(F32))[:, 0:n_g + n_new]
    take_x = expand(take)[:, 0:n_g + n_new]
    head = lax.broadcasted_iota(I32, (t, n_g + n_new), 1) & (N_HEADS - 1)
    in_range = lax.broadcasted_iota(I32, (t, n_g + n_new), 1) < n_g + t * N_HEADS
    bias = jnp.zeros((t, n_g + n_new), F32)
    for h in range(N_HEADS):
        bias = jnp.where(head == h, _bias_of_bucket(bucket_x, relb_ref, h), bias)
    amask = jnp.where((take_x > 0.5) & in_range, 0.0, NEG_BIG)

    qf = q_ref[0].astype(F32)
    pad = jnp.zeros((LANES - t * N_HEADS, HEAD_DIM), F32)
    k_new = jnp.concatenate([kn_ref[0], pad], axis=0).astype(BF16)
    v_new = jnp.concatenate([vn_ref[0], pad], axis=0).astype(BF16)

    def own_head(n):
        return (lax.broadcasted_iota(I32, (N_HEADS, n), 0)
                == (lax.broadcasted_iota(I32, (N_HEADS, n), 1) & (N_HEADS - 1)))

    def gathered(buf, qq):
        return [buf[(qq * sel_pages + s) * PAGE_ROWS:(qq * sel_pages + s + 1) * PAGE_ROWS, :].astype(BF16)
                for s in range(sel_pages)]

    n_rows = t * topk * N_HEADS
    pltpu.make_async_copy(ck_hbm.at[pl.ds(0, n_rows)], kg, ksem).wait()
    lg_rows = []
    for qq in range(t):
        q_heads = jnp.concatenate([qf[qq:qq + 1, h * HEAD_DIM:(h + 1) * HEAD_DIM] for h in range(N_HEADS)],
                                  axis=0).astype(BF16)
        parts = []
        for x_bf in gathered(kg, qq) + [k_new]:
            r = lax.dot_general(q_heads, x_bf, nt_dims, preferred_element_type=F32)
            parts.append(jnp.sum(jnp.where(own_head(r.shape[1]), r, 0.0), axis=0, keepdims=True))
        lg_rows.append(jnp.concatenate(parts, axis=1))
    lg = jnp.concatenate(lg_rows, axis=0) * scale + bias + amask

    def per_head_allreduce(x, op):
        for shift in (8, 16, 32, 64):
            x = op(x, pltpu.roll(x, shift, axis=1))
        return x

    groups = (n_g + n_new) // LANES
    m_run = lg[:, 0:LANES]
    for s in range(1, groups):
        m_run = jnp.maximum(m_run, lg[:, s * LANES:(s + 1) * LANES])
    m_fin = per_head_allreduce(m_run, jnp.maximum)
    pr = jnp.exp(lg - jnp.concatenate([m_fin] * groups, axis=1))
    l_run = pr[:, 0:LANES]
    for s in range(1, groups):
        l_run = l_run + pr[:, s * LANES:(s + 1) * LANES]
    l_fin = per_head_allreduce(l_run, jnp.add)

    pltpu.make_async_copy(cv_hbm.at[pl.ds(0, n_rows)], vg, vsem).wait()
    for qq in range(t):
        acc = jnp.zeros((N_HEADS, HEAD_DIM), F32)
        lane0 = 0
        for x_bf in gathered(vg, qq) + [v_new]:
            n = x_bf.shape[0]
            p_heads = jnp.where(own_head(n), jnp.broadcast_to(pr[qq:qq + 1, lane0:lane0 + n], (N_HEADS, n)), 0.0)
            acc = acc + jnp.dot(p_heads.astype(BF16), x_bf, preferred_element_type=F32)
            lane0 += n
        acc_s[qq * N_HEADS:(qq + 1) * N_HEADS, :] = acc
    for h in range(N_HEADS):
        o_ref[0, :, h * HEAD_DIM:(h + 1) * HEAD_DIM] = (
            acc_s[pl.ds(h, t, stride=N_HEADS), :] / l_fin[:, h:h + 1]).astype(BF16)


def _sample_gather_attn(page_table, rel_bias, qb, qib, kw, k_new, v_new, cki, ck, cv, topk, page_base):
    b, t, _ = qb.shape
    n_pages = page_table.shape[1]
    past_len = n_pages * PAGE
    nqi = N_IDX_HEADS * IDX_DIM
    assert t == SUBLANES and n_pages == PAGE and topk % PAGE == 0 and n_pages % IDX_PAGES == 0
    per_b = lambda bi, pt: (bi, 0, 0)
    grid_spec = pltpu.PrefetchScalarGridSpec(
        num_scalar_prefetch=1,
        grid=(b,),
        in_specs=[pl.BlockSpec(memory_space=pltpu.SMEM),
                  pl.BlockSpec((1, 1, n_pages), per_b),
                  pl.BlockSpec((1, t, D_MODEL), per_b),
                  pl.BlockSpec((1, t, nqi), per_b),
                  pl.BlockSpec((1, t, LANES), per_b),
                  pl.BlockSpec((1, t * N_HEADS, HEAD_DIM), per_b),
                  pl.BlockSpec((1, t * N_HEADS, HEAD_DIM), per_b),
                  pl.BlockSpec(memory_space=pl.ANY),
                  pl.BlockSpec(memory_space=pl.ANY),
                  pl.BlockSpec(memory_space=pl.ANY)],
        out_specs=pl.BlockSpec((1, t, D_MODEL), per_b),
        scratch_shapes=[pltpu.VMEM((n_pages, IDX_DIM, PAGE), F32),
                        pltpu.VMEM((t * topk * N_HEADS, HEAD_DIM), F32),
                        pltpu.VMEM((t * topk * N_HEADS, HEAD_DIM), F32),
                        pltpu.VMEM((n_pages + 1, t, PAGE), I32),
                        pltpu.VMEM((t, topk), I32),
                        pltpu.VMEM((t, topk), I32),
                        pltpu.SMEM((t, topk), I32),
                        pltpu.VMEM((t * N_HEADS, HEAD_DIM), F32),
                        pltpu.SemaphoreType.DMA((n_pages // IDX_PAGES,)),
                        pltpu.SemaphoreType.DMA(()),
                        pltpu.SemaphoreType.DMA(()),
                        pltpu.SemaphoreType.DMA(())])
    return pl.pallas_call(
        functools.partial(_sample_gather_kernel, t=t, past_len=past_len, topk=topk, page_base=page_base),
        grid_spec=grid_spec,
        out_shape=jax.ShapeDtypeStruct((b, t, D_MODEL), BF16),
        compiler_params=_params(("arbitrary",)),
        name="sample_attn",
    )(page_table, rel_bias, page_table.reshape(b, 1, n_pages), qb, qib, kw, k_new, v_new, cki, ck, cv)


def kernel(x_prompt, x_sample, state_conv_a, state_conv_b, cache_k, cache_v, cache_kidx, page_table,
           norm_conv, w_in_conv, conv_a_w, conv_b_w, conv_b_bias, ln_b_gain, ln_b_bias, w_out_conv,
           norm_attn, w_in_attn, w_out_attn, rel_bias, norm_mlp, w_up, w_down, norm_final):
    depth = norm_mlp.shape[0]
    n_pool = cache_k.shape[1]
    past_len = page_table.shape[1] * PAGE
    nqkv = 3 * D_MODEL
    nqi = N_IDX_HEADS * IDX_DIM
    row = lambda a: a.reshape(1, -1)

    cki_flat = jnp.swapaxes(cache_kidx, 2, 3).reshape(-1, IDX_DIM, PAGE)
    ck_flat = cache_k.reshape(-1, HEAD_DIM)
    cv_flat = cache_v.reshape(-1, HEAD_DIM)
    dbias = _prompt_bias(rel_bias)

    def run(x, sample):
        b, t, _ = x.shape
        m = b * t
        tm = min(512, m)
        x2d = x.reshape(m, D_MODEL)
        new_a, new_b, new_k, new_v, new_ki = [], [], [], [], []
        for layer in range(depth):
            i = layer // 2
            last = layer == depth - 1
            if layer % 2 == 0:
                if sample:
                    buf_a, buf_b = state_conv_a[i], state_conv_b[i]
                else:
                    buf_a = jnp.zeros((b, CONV_A_WIDTH - 1, D_A), F32)
                    buf_b = jnp.zeros((b, CONV_B_WIDTH - 1, D_B), F32)
                sa = jnp.pad(buf_a, ((0, 0), (A_HALO - (CONV_A_WIDTH - 1), 0), (0, 0)))
                sb = jnp.pad(buf_b, ((0, 0), (B_HALO - (CONV_B_WIDTH - 1), 0), (0, 0)))
                gb, cx, u = _conv_in(x2d, row(norm_conv[i]), w_in_conv[i].astype(BF16), tm)
                tt = min(512, t)
                y, na, nb = _conv(gb.reshape(b, t, D_A), cx.reshape(b, t, D_A), u.reshape(b, t, D_B), sa, sb,
                                  conv_a_w[i], conv_b_w[i], row(conv_b_bias[i]), row(ln_b_gain[i]),
                                  row(ln_b_bias[i]), tt, min(64, tt))
                new_a.append(na)
                new_b.append(nb)
                mix = y.reshape(m, D_MODEL)
                w_out = w_out_conv[i]
            else:
                w_in = w_in_attn[i]
                wkw = jnp.pad(w_in[:, nqkv + nqi:], ((0, 0), (0, LANES - IDX_DIM - N_IDX_HEADS)))
                wvt = w_in[:, 2 * D_MODEL:nqkv].T.astype(BF16)
                outs = _attn_in(x2d, row(norm_attn[i]), w_in[:, :nqkv].astype(BF16), wvt,
                                w_in[:, nqkv:nqkv + nqi].astype(BF16), wkw.astype(BF16), tm, t, not sample)
                qb, k, v, qib, kw = outs[:5]
                shp = lambda a: a.reshape(b, t, a.shape[-1])
                if sample:
                    topk = min(TOPK_MAX, (past_len + t) // 4)
                    o = _sample_gather_attn(page_table, rel_bias, shp(qb), shp(qib), shp(kw),
                                            k.reshape(b, t * N_HEADS, HEAD_DIM),
                                            v.reshape(b, t * N_HEADS, HEAD_DIM),
                                            cki_flat, ck_flat, cv_flat, topk, i * n_pool)
                else:
                    topk = min(TOPK_MAX, t // 4)
                    kb, vt = outs[5:]
                    o = _prompt_attn(rel_bias, shp(qb), shp(qib), shp(kw), shp(kb), vt, dbias, topk)
                new_k.append(k.reshape(b, t, N_HEADS, HEAD_DIM))
                new_v.append(v.reshape(b, t, N_HEADS, HEAD_DIM))
                new_ki.append(kw[:, :IDX_DIM].reshape(b, t, IDX_DIM))
                mix = o.reshape(m, D_MODEL)
                w_out = w_out_attn[i]
            x2d = _out_mlp(mix, x2d, w_out.astype(BF16), row(norm_mlp[layer]), w_up[layer].astype(BF16),
                           w_down[layer].astype(BF16), row(norm_final), tm, last)
        return (x2d.reshape(b, t, D_MODEL), jnp.stack(new_a), jnp.stack(new_b),
                jnp.stack(new_k), jnp.stack(new_v), jnp.stack(new_ki))

    y_p, ca_p, cb_p, k_p, v_p, ki_p = run(x_prompt, False)
    y_s, ca_s, cb_s, k_s, v_s, ki_s = run(x_sample, True)
    return (y_p, y_s, ca_p, cb_p, k_p, v_p, ki_p, ca_s, cb_s, k_s, v_s, ki_s)
```

```python
import functools
import math

import jax
import jax.numpy as jnp
from jax import lax
from jax.experimental import pallas as pl
from jax.experimental.pallas import tpu as pltpu

F32 = jnp.float32
BF16 = jnp.bfloat16
I32 = jnp.int32

D_MODEL = 1024
D_A = 512
D_B = 512
CONV_A_WIDTH = 3
CONV_B_WIDTH = 31
N_HEADS = 8
HEAD_DIM = 128
N_IDX_HEADS = 8
IDX_DIM = 64
TOPK_MAX = 256
NUM_BUCKETS = 32
MAX_DISTANCE = 128
D_FF = 4096
EPS = 1e-6
PAGE = 128
LANES = 128
SUBLANES = 8

INT_MIN = -(2 ** 31)
INT_MAX = 2 ** 31 - 1
NEG_BIG = -1e30

VMEM_LIMIT_BYTES = 56 * 1024 * 1024

A_HALO = 8
B_HALO = 32

TQ = 256
TK = 128
NEAR_TILES = TQ // TK + 1
def _first_far_distance():
    exact = NUM_BUCKETS // 2
    d = exact
    while exact + int(math.log(d / exact) / math.log(MAX_DISTANCE / exact) * (NUM_BUCKETS - exact)) < NUM_BUCKETS - 1:
        d += 1
    return d


FAR_DISTANCE = _first_far_distance()
LOG2E = math.log2(math.e)
QK_SCALE = (HEAD_DIM ** -0.5) * LOG2E
IDX_PAGES = 16
IDX_CHUNK = IDX_PAGES * PAGE
PAGE_ROWS = PAGE * N_HEADS


def _params(sem):
    return pltpu.CompilerParams(dimension_semantics=sem, vmem_limit_bytes=VMEM_LIMIT_BYTES)


def _const_spec(shape):
    nd = len(shape)
    return pl.BlockSpec(shape, lambda *_: (0,) * nd, pipeline_mode=pl.Buffered(1))


def _rms_bf16(x, g):
    ms = jnp.mean(x * x, axis=-1, keepdims=True)
    return ((x * lax.rsqrt(ms + EPS)) * g).astype(BF16)


def _sigmoid(x):
    return 1.0 / (1.0 + jnp.exp(-x))


def _conv_in_kernel(x_ref, g_ref, w_ref, gb_ref, cx_ref, u_ref):
    xn = _rms_bf16(x_ref[...], g_ref[...])
    proj = jnp.dot(xn, w_ref[...], preferred_element_type=F32)
    gb_ref[...] = proj[:, :D_A]
    cx_ref[...] = proj[:, D_A:2 * D_A] * proj[:, 2 * D_A:3 * D_A]
    u_ref[...] = proj[:, 3 * D_A:3 * D_A + D_B] * _sigmoid(proj[:, 3 * D_A + D_B:])


def _conv_in(x2d, g, w_bf, tm):
    m = x2d.shape[0]
    n = w_bf.shape[1]
    out = jax.ShapeDtypeStruct((m, D_A), F32)
    return pl.pallas_call(
        _conv_in_kernel,
        grid=(m // tm,),
        in_specs=[pl.BlockSpec((tm, D_MODEL), lambda i: (i, 0)),
                  _const_spec((1, D_MODEL)),
                  _const_spec((D_MODEL, n))],
        out_specs=[pl.BlockSpec((tm, D_A), lambda i: (i, 0))] * 3,
        out_shape=[out, out, out],
        compiler_params=_params(("parallel",)),
        name="conv_in",
    )(x2d, g, w_bf)


def _conv_kernel(gb_ref, cx_ref, u_ref, sa_ref, sb_ref, wa_ref, wb_ref, bias_ref, lng_ref, lnb_ref,
                 y_ref, na_ref, nb_ref, xa_s, xb_s, sh_s, *, tt, rc):
    t = pl.program_id(1)

    @pl.when(t == 0)
    def _():
        xa_s[0:A_HALO, :] = sa_ref[0]
        xb_s[0:B_HALO, :] = sb_ref[0]

    @pl.when(t > 0)
    def _():
        xa_s[0:A_HALO, :] = xa_s[tt:tt + A_HALO, :]
        xb_s[0:B_HALO, :] = xb_s[tt:tt + B_HALO, :]

    xa_s[A_HALO:A_HALO + tt, :] = cx_ref[0]
    xb_s[B_HALO:B_HALO + tt, :] = u_ref[0]

    bias = bias_ref[...]
    lng = lng_ref[...]
    lnb = lnb_ref[...]
    a0 = A_HALO - (CONV_A_WIDTH - 1)
    b0 = B_HALO - (CONV_B_WIDTH - 1)
    for c in range(tt // rc):
        r0 = c * rc
        ya = wa_ref[0:1, :] * xa_s[a0 + r0:a0 + r0 + rc, :]
        for j in range(1, CONV_A_WIDTH):
            ya = ya + wa_ref[j:j + 1, :] * xa_s[a0 + j + r0:a0 + j + r0 + rc, :]
        y_ref[0, r0:r0 + rc, 0:D_A] = gb_ref[0, r0:r0 + rc, :] * ya

        yb = None
        for shift in range(SUBLANES):
            offs = [o for o in range(b0, b0 + CONV_B_WIDTH) if o % SUBLANES == shift]
            if not offs:
                continue
            rows_needed = offs[-1] - shift + rc
            sh_s[shift, 0:rows_needed, :] = xb_s[r0 + shift:r0 + shift + rows_needed, :]
            for o in offs:
                term = wb_ref[o - b0:o - b0 + 1, :] * sh_s[shift, o - shift:o - shift + rc, :]
                yb = term if yb is None else yb + term
        yb = yb + bias
        mu = jnp.mean(yb, axis=-1, keepdims=True)
        var = jnp.mean(jnp.square(yb - mu), axis=-1, keepdims=True)
        z = (yb - mu) * lax.rsqrt(var + EPS) * lng + lnb
        y_ref[0, r0:r0 + rc, D_A:D_A + D_B] = z * _sigmoid(z)

    na_ref[0] = xa_s[tt + A_HALO - (CONV_A_WIDTH - 1):tt + A_HALO, :]
    nb_ref[0] = xb_s[tt + B_HALO - (CONV_B_WIDTH - 1):tt + B_HALO, :]


def _conv(gb, cx, u, sa, sb, wa, wb, bias, lng, lnb, tt, rc):
    b, t, _ = cx.shape
    row = lambda i, j: (i, j, 0)
    per_b = lambda i, j: (i, 0, 0)
    return pl.pallas_call(
        functools.partial(_conv_kernel, tt=tt, rc=rc),
        grid=(b, t // tt),
        in_specs=[pl.BlockSpec((1, tt, D_A), row),
                  pl.BlockSpec((1, tt, D_A), row),
                  pl.BlockSpec((1, tt, D_B), row),
                  pl.BlockSpec((1, A_HALO, D_A), per_b),
                  pl.BlockSpec((1, B_HALO, D_B), per_b),
                  pl.BlockSpec((CONV_A_WIDTH, D_A), lambda i, j: (0, 0)),
                  pl.BlockSpec((CONV_B_WIDTH, D_B), lambda i, j: (0, 0)),
                  pl.BlockSpec((1, D_B), lambda i, j: (0, 0)),
                  pl.BlockSpec((1, D_B), lambda i, j: (0, 0)),
                  pl.BlockSpec((1, D_B), lambda i, j: (0, 0))],
        out_specs=[pl.BlockSpec((1, tt, D_MODEL), row),
                   pl.BlockSpec((1, CONV_A_WIDTH - 1, D_A), per_b),
                   pl.BlockSpec((1, CONV_B_WIDTH - 1, D_B), per_b)],
        out_shape=[jax.ShapeDtypeStruct((b, t, D_MODEL), F32),
                   jax.ShapeDtypeStruct((b, CONV_A_WIDTH - 1, D_A), F32),
                   jax.ShapeDtypeStruct((b, CONV_B_WIDTH - 1, D_B), F32)],
        scratch_shapes=[pltpu.VMEM((A_HALO + tt, D_A), F32),
                        pltpu.VMEM((B_HALO + tt, D_B), F32),
                        pltpu.VMEM((SUBLANES, B_HALO + rc, D_B), F32)],
        compiler_params=_params(("parallel", "arbitrary")),
        name="conv_mix",
    )(gb, cx, u, sa, sb, wa, wb, bias, lng, lnb)


def _out_mlp_kernel(a_ref, res_ref, wo_ref, g_ref, wu_ref, wd_ref, gf_ref, o_ref, *, final_norm, ff_chunk):
    x1 = res_ref[...] + jnp.dot(a_ref[...].astype(BF16), wo_ref[...], preferred_element_type=F32)
    xn = _rms_bf16(x1, g_ref[...])
    acc = x1
    for c in range(D_FF // ff_chunk):
        h = jnp.dot(xn, wu_ref[:, c * ff_chunk:(c + 1) * ff_chunk], preferred_element_type=F32)
        h = jnp.square(jnp.maximum(h, 0.0)).astype(BF16)
        acc = acc + jnp.dot(h, wd_ref[c * ff_chunk:(c + 1) * ff_chunk, :], preferred_element_type=F32)
    if final_norm:
        ms = jnp.mean(acc * acc, axis=-1, keepdims=True)
        acc = (acc * lax.rsqrt(ms + EPS)) * gf_ref[...]
    o_ref[...] = acc


def _out_mlp(a2d, res2d, wo_bf, g, wu_bf, wd_bf, gf, tm, final_norm):
    m = res2d.shape[0]
    row = lambda i: (i, 0)
    return pl.pallas_call(
        functools.partial(_out_mlp_kernel, final_norm=final_norm, ff_chunk=1024),
        grid=(m // tm,),
        in_specs=[pl.BlockSpec((tm, D_MODEL), row),
                  pl.BlockSpec((tm, D_MODEL), row),
                  _const_spec((D_MODEL, D_MODEL)),
                  _const_spec((1, D_MODEL)),
                  _const_spec((D_MODEL, D_FF)),
                  _const_spec((D_FF, D_MODEL)),
                  _const_spec((1, D_MODEL))],
        out_specs=pl.BlockSpec((tm, D_MODEL), row),
        out_shape=jax.ShapeDtypeStruct((m, D_MODEL), F32),
        compiler_params=_params(("parallel",)),
        name="out_mlp",
    )(a2d, res2d, wo_bf, g, wu_bf, wd_bf, gf)


def _attn_in_kernel(*refs, prompt, tm):
    if prompt:
        (x_ref, g_ref, wqkv_ref, wvt_ref, wqi_ref, wkw_ref,
         qb_ref, k_ref, v_ref, qib_ref, kw_ref, kb_ref, vt_ref) = refs
    else:
        x_ref, g_ref, wqkv_ref, wqi_ref, wkw_ref, qb_ref, k_ref, v_ref, qib_ref, kw_ref = refs
    xn = _rms_bf16(x_ref[...], g_ref[...])
    q = jnp.dot(xn, wqkv_ref[:, 0:D_MODEL], preferred_element_type=F32)
    qb_ref[...] = (q * QK_SCALE if prompt else q).astype(BF16)
    k = jnp.dot(xn, wqkv_ref[:, D_MODEL:2 * D_MODEL], preferred_element_type=F32)
    v = jnp.dot(xn, wqkv_ref[:, 2 * D_MODEL:3 * D_MODEL], preferred_element_type=F32)
    for h in range(N_HEADS):
        k_ref[:, h, :] = k[:, h * HEAD_DIM:(h + 1) * HEAD_DIM]
        v_ref[:, h, :] = v[:, h * HEAD_DIM:(h + 1) * HEAD_DIM]
    qib_ref[...] = jnp.dot(xn, wqi_ref[...], preferred_element_type=F32).astype(BF16)
    kw_ref[...] = jnp.dot(xn, wkw_ref[...], preferred_element_type=F32)
    if prompt:
        kb_ref[...] = k.astype(BF16)
        vt = lax.dot_general(wvt_ref[...], xn, (((1,), (1,)), ((), ())),
                             preferred_element_type=F32).astype(BF16)
        for s in range(tm // TK):
            vt_ref[0, s] = vt[:, s * TK:(s + 1) * TK]


def _attn_in(x2d, g, wqkv_bf, wvt_bf, wqi_bf, wkw_bf, tm, seq_len, prompt):
    m = x2d.shape[0]
    row = lambda i: (i, 0)
    row3 = lambda i: (i, 0, 0)
    nqi = N_IDX_HEADS * IDX_DIM
    in_specs = [pl.BlockSpec((tm, D_MODEL), row),
                _const_spec((1, D_MODEL)),
                _const_spec((D_MODEL, 3 * D_MODEL))]
    args = [x2d, g, wqkv_bf]
    if prompt:
        in_specs.append(_const_spec((D_MODEL, D_MODEL)))
        args.append(wvt_bf)
    in_specs += [_const_spec((D_MODEL, nqi)), _const_spec((D_MODEL, LANES))]
    args += [wqi_bf, wkw_bf]
    out_specs = [pl.BlockSpec((tm, D_MODEL), row),
                 pl.BlockSpec((tm, N_HEADS, HEAD_DIM), row3),
                 pl.BlockSpec((tm, N_HEADS, HEAD_DIM), row3),
                 pl.BlockSpec((tm, nqi), row),
                 pl.BlockSpec((tm, LANES), row)]
    out_shape = [jax.ShapeDtypeStruct((m, D_MODEL), BF16),
                 jax.ShapeDtypeStruct((m, N_HEADS, HEAD_DIM), F32),
                 jax.ShapeDtypeStruct((m, N_HEADS, HEAD_DIM), F32),
                 jax.ShapeDtypeStruct((m, nqi), BF16),
                 jax.ShapeDtypeStruct((m, LANES), F32)]
    if prompt:
        steps_per_seq = seq_len // tm
        out_specs += [pl.BlockSpec((tm, D_MODEL), row),
                      pl.BlockSpec((1, tm // TK, D_MODEL, TK),
                                   lambda i: (i // steps_per_seq, i % steps_per_seq, 0, 0))]
        out_shape += [jax.ShapeDtypeStruct((m, D_MODEL), BF16),
                      jax.ShapeDtypeStruct((m // seq_len, seq_len // TK, D_MODEL, TK), BF16)]
    return pl.pallas_call(
        functools.partial(_attn_in_kernel, prompt=prompt, tm=tm),
        grid=(m // tm,),
        in_specs=in_specs,
        out_specs=out_specs,
        out_shape=out_shape,
        compiler_params=_params(("parallel",)),
        name="attn_in",
    )(*args)


def _t5_bucket(d):
    n = jnp.maximum(d, 0)
    max_exact = NUM_BUCKETS // 2
    nf = jnp.maximum(n, 1).astype(F32)
    large = max_exact + (jnp.log(nf / max_exact) / math.log(MAX_DISTANCE / max_exact)
                         * (NUM_BUCKETS - max_exact)).astype(I32)
    large = jnp.minimum(large, NUM_BUCKETS - 1)
    return jnp.where(n < max_exact, n, large)


def _bias_of_bucket(bucket, relb_ref, h):
    val = jnp.full(bucket.shape, relb_ref[NUM_BUCKETS - 1, h], F32)
    for bk in range(NUM_BUCKETS - 2, -1, -1):
        val = jnp.where(bucket == bk, relb_ref[bk, h], val)
    return val


def _prompt_bias_kernel(relb_ref, o_ref):
    h = pl.program_id(0)
    c = lax.broadcasted_iota(I32, (NEAR_TILES * TK, TQ), 0)
    r = lax.broadcasted_iota(I32, (NEAR_TILES * TK, TQ), 1)
    o_ref[0] = _bias_of_bucket(_t5_bucket(r + TK - c), relb_ref, h) * LOG2E


def _prompt_bias(rel_bias):
    return pl.pallas_call(
        _prompt_bias_kernel,
        grid=(N_HEADS,),
        in_specs=[pl.BlockSpec(memory_space=pltpu.SMEM)],
        out_specs=pl.BlockSpec((1, NEAR_TILES * TK, TQ), lambda h: (h, 0, 0)),
        out_shape=jax.ShapeDtypeStruct((N_HEADS, NEAR_TILES * TK, TQ), F32),
        compiler_params=_params(("parallel",)),
        name="prompt_bias",
    )(rel_bias)


def _sortable(score):
    bits = pltpu.bitcast(score, I32)
    return jnp.where(bits < 0, bits ^ INT_MAX, bits)


def _bisect(count_fn, n_bits, init, shape):
    def body(it, cur):
        trial = cur ^ lax.shift_left(jnp.int32(1), jnp.asarray(n_bits - 1 - it, I32))
        return jnp.where(count_fn(trial), trial, cur)
    return lax.fori_loop(0, n_bits, body, jnp.full(shape, init, I32))


def _bisect_pairs(ok_fn, n_bits, init, shape):
    def body(it, cur):
        hi = lax.shift_left(jnp.int32(1), jnp.asarray(n_bits - 1 - 2 * it, I32))
        lo = lax.shift_left(jnp.int32(1), jnp.asarray(n_bits - 2 - 2 * it, I32))
        t_hi, t_lo = cur ^ hi, cur ^ lo
        t_both = t_hi ^ lo
        return jnp.where(ok_fn(t_both), t_both, jnp.where(ok_fn(t_hi), t_hi, jnp.where(ok_fn(t_lo), t_lo, cur)))
    return lax.fori_loop(0, n_bits // 2, body, jnp.full(shape, init, I32))


def _prompt_attn_kernel(relb_ref, q_ref, qi_ref, kwq_ref, kw_ref, k_ref, vt_ref, db_ref, o_ref,
                        keys_s, acc_s, m_s, l_s, *, topk):
    i = pl.program_id(1)
    tiles_per_q = TQ // TK
    nt = (i + 1) * tiles_per_q
    nt_dims = (((1,), (1,)), ((), ()))
    row = (1, TQ)

    qi = qi_ref[0]
    qi_h = [qi[:, h * IDX_DIM:(h + 1) * IDX_DIM] for h in range(N_IDX_HEADS)]
    kwq_t = kwq_ref[0].T
    w_scale = (N_IDX_HEADS ** -0.5) * (IDX_DIM ** -0.5)
    w_h = [kwq_t[IDX_DIM + h:IDX_DIM + h + 1, :] * w_scale for h in range(N_IDX_HEADS)]
    krow = lax.broadcasted_iota(I32, (TK, TQ), 0)
    qpos = i * TQ + lax.broadcasted_iota(I32, (TK, TQ), 1)

    def score_chunk(jj, _):
        r0 = pl.multiple_of(jj * TQ, TQ)
        ki_t = kw_ref[0, pl.ds(r0, TQ), 0:IDX_DIM].astype(BF16)
        score = jnp.zeros((TQ, TQ), F32)
        for h in range(N_IDX_HEADS):
            s = lax.dot_general(ki_t, qi_h[h], nt_dims, preferred_element_type=F32)
            score = score + jnp.maximum(s, 0.0) * w_h[h]
        key = _sortable(score)
        for s in range(tiles_per_q):
            j = jj * tiles_per_q + s
            keys_s[j] = jnp.where(j * TK + krow <= qpos, key[s * TK:(s + 1) * TK, :], INT_MIN)
        return 0

    lax.fori_loop(0, i + 1, score_chunk, 0)

    def count(pred_fn):
        def tile_count(j):
            m = jnp.where(pred_fn(keys_s[j], j), 1, 0).astype(I32)
            return jnp.sum(m.reshape(TK // SUBLANES, SUBLANES, TQ), axis=0)

        def body(jj, c):
            for s in range(tiles_per_q):
                c = c + tile_count(jj * tiles_per_q + s)
            return c
        c = lax.fori_loop(0, i + 1, body, jnp.zeros((SUBLANES, TQ), I32))
        return jnp.sum(c, axis=0, keepdims=True)

    thr = _bisect(lambda x: count(lambda kt, j: kt >= x) >= topk, 32, INT_MIN, row)
    n_gt = count(lambda kt, j: kt > thr)
    thr1 = jnp.maximum(thr, INT_MIN + 1)
    n_eq = count(lambda kt, j: kt == thr1)
    need = topk - n_gt

    def tie_bound():
        return _bisect(lambda x: count(lambda kt, j: jnp.where(kt == thr, j * TK + krow, INT_MAX) < x) < need,
                       31, 0, row)

    has_tie = jnp.max(n_eq - need) > 0
    xb = lax.cond(has_tie, tie_bound, lambda: jnp.full(row, INT_MAX, I32))

    q = q_ref[0]
    q_h = [q[:, h * HEAD_DIM:(h + 1) * HEAD_DIM] for h in range(N_HEADS)]
    acc_s[...] = jnp.zeros_like(acc_s)
    m_s[...] = jnp.full_like(m_s, NEG_BIG)
    l_s[...] = jnp.zeros_like(l_s)

    def attend(j, near):
        r0 = pl.multiple_of(j * TK, TK)
        kt = keys_s[j]
        tied = jnp.where(j * TK + krow <= xb, 0.0, -jnp.inf)
        amask = jnp.where(kt > thr1, 0.0, jnp.where(kt == thr1, tied, -jnp.inf))
        for h in range(N_HEADS):
            hs = slice(h * HEAD_DIM, (h + 1) * HEAD_DIM)
            k_t = k_ref[0, pl.ds(r0, TK), hs]
            s = lax.dot_general(k_t, q_h[h], nt_dims, preferred_element_type=F32)
            m_old = m_s[h:h + 1, :]
            if near is None:
                c = relb_ref[NUM_BUCKETS - 1, h] * LOG2E
                lg = s + amask
                m_new = jnp.maximum(m_old, jnp.max(lg, axis=0, keepdims=True) + c)
                p = jnp.exp2(lg - (m_new - c))
            else:
                lg = s + (amask + db_ref[h, near * TK:(near + 1) * TK, :])
                m_new = jnp.maximum(m_old, jnp.max(lg, axis=0, keepdims=True))
                p = jnp.exp2(lg - m_new)
            alpha = jnp.exp2(m_old - m_new)
            l_s[h:h + 1, :] = alpha * l_s[h:h + 1, :] + jnp.sum(p, axis=0, keepdims=True)
            m_s[h:h + 1, :] = m_new
            pv = jnp.dot(vt_ref[0, j, hs, :], p.astype(BF16), preferred_element_type=F32)
            acc_s[hs, :] = alpha * acc_s[hs, :] + pv

    first_near = i * tiles_per_q - 1

    def far_quad(jj, _):
        for s in range(4):
            attend(4 * jj + s, None)
        return 0

    n_far = jnp.maximum(first_near, 0)
    lax.fori_loop(0, lax.shift_right_logical(n_far, 2), far_quad, 0)
    rest = n_far & ~3

    @pl.when((n_far & 2) == 2)
    def _():
        attend(rest, None)
        attend(rest + 1, None)

    @pl.when((n_far & 1) == 1)
    def _():
        attend(n_far - 1, None)

    @pl.when(i >= 1)
    def _():
        attend(first_near, 0)

    for near in range(1, NEAR_TILES):
        attend(first_near + near, near)

    for h in range(N_HEADS):
        hs = slice(h * HEAD_DIM, (h + 1) * HEAD_DIM)
        o_ref[0, :, hs] = (acc_s[hs, :] / l_s[h:h + 1, :]).T.astype(BF16)


def _prompt_attn(rel_bias, qb, qib, kw, kb, vt, dbias, topk):
    b, t, _ = qb.shape
    nqi = N_IDX_HEADS * IDX_DIM
    assert TK + 1 >= FAR_DISTANCE and TQ % TK == 0
    return pl.pallas_call(
        functools.partial(_prompt_attn_kernel, topk=topk),
        grid=(b, t // TQ),
        in_specs=[pl.BlockSpec(memory_space=pltpu.SMEM),
                  pl.BlockSpec((1, TQ, D_MODEL), lambda bi, i: (bi, i, 0)),
                  pl.BlockSpec((1, TQ, nqi), lambda bi, i: (bi, i, 0)),
                  pl.BlockSpec((1, TQ, LANES), lambda bi, i: (bi, i, 0)),
                  pl.BlockSpec((1, t, LANES), lambda bi, i: (bi, 0, 0)),
                  pl.BlockSpec((1, t, D_MODEL), lambda bi, i: (bi, 0, 0)),
                  pl.BlockSpec((1, t // TK, D_MODEL, TK), lambda bi, i: (bi, 0, 0, 0)),
                  _const_spec((N_HEADS, NEAR_TILES * TK, TQ))],
        out_specs=pl.BlockSpec((1, TQ, D_MODEL), lambda bi, i: (bi, i, 0)),
        out_shape=jax.ShapeDtypeStruct((b, t, D_MODEL), BF16),
        scratch_shapes=[pltpu.VMEM((t // TK, TK, TQ), I32),
                        pltpu.VMEM((D_MODEL, TQ), F32),
                        pltpu.VMEM((N_HEADS, TQ), F32),
                        pltpu.VMEM((N_HEADS, TQ), F32)],
        compiler_params=_params(("parallel", "arbitrary")),
        name="prompt_attn",
    )(rel_bias, qb, qib, kw, kw, kb, vt, dbias)


def _sample_gather_kernel(pt_ref, relb_ref, ptv_ref, q_ref, qi_ref, kw_ref, kn_ref, vn_ref,
                          cki_hbm, ck_hbm, cv_hbm,
                          o_ref, ibuf, kg, vg, keys_s, idx_v, rows_v, idx_sm, acc_s, isem, ksem, vsem, ssem,
                          *, t, past_len, topk, page_base):
    b = pl.program_id(0)
    n_idx_chunks = past_len // IDX_CHUNK
    n_pages = past_len // PAGE
    n_tiles = n_pages + 1
    sel_pages = topk // PAGE
    scale = HEAD_DIM ** -0.5
    nt_dims = (((1,), (1,)), ((), ()))

    def idx_copy(c, p):
        page = page_base + pt_ref[b, c * IDX_PAGES + p]
        return pltpu.make_async_copy(cki_hbm.at[page], ibuf.at[c * IDX_PAGES + p], isem.at[c])

    for c in range(n_idx_chunks):
        for p in range(IDX_PAGES):
            idx_copy(c, p).start()

    qi = qi_ref[0].astype(F32)
    qi_rows = jnp.concatenate([qi[:, h * IDX_DIM:(h + 1) * IDX_DIM] for h in range(N_IDX_HEADS)],
                              axis=0).astype(BF16)
    w = kw_ref[0][:, IDX_DIM:IDX_DIM + N_IDX_HEADS] * (N_IDX_HEADS ** -0.5) * (IDX_DIM ** -0.5)
    w_h = [w[:, h:h + 1] for h in range(N_IDX_HEADS)]

    def scores(ki_bf, transposed):
        if transposed:
            s = jnp.dot(qi_rows, ki_bf, preferred_element_type=F32)
        else:
            s = lax.dot_general(qi_rows, ki_bf, nt_dims, preferred_element_type=F32)
        score = jnp.maximum(s[0:t], 0.0) * w_h[0]
        for h in range(1, N_IDX_HEADS):
            score = score + jnp.maximum(s[h * t:(h + 1) * t], 0.0) * w_h[h]
        return score

    for c in range(n_idx_chunks):
        for p in range(IDX_PAGES):
            idx_copy(c, p).wait()
        ki_t = jnp.concatenate([ibuf[c * IDX_PAGES + p] for p in range(IDX_PAGES)], axis=1)
        key = _sortable(scores(ki_t.astype(BF16), True))
        for k in range(IDX_PAGES):
            keys_s[c * IDX_PAGES + k] = key[:, k * PAGE:(k + 1) * PAGE]

    ki_new = jnp.concatenate([kw_ref[0][:, 0:IDX_DIM], jnp.zeros((PAGE - t, IDX_DIM), F32)], axis=0)
    qrow = lax.broadcasted_iota(I32, (t, PAGE), 0)
    lane = lax.broadcasted_iota(I32, (t, PAGE), 1)
    keys_s[n_tiles - 1] = jnp.where(lane <= qrow, _sortable(scores(ki_new.astype(BF16), False)), INT_MIN)

    pos = (lax.broadcasted_iota(I32, (n_tiles, t, PAGE), 0) * PAGE
           + lax.broadcasted_iota(I32, (n_tiles, t, PAGE), 2))

    def count(pred):
        m = jnp.where(pred, 1, 0).astype(I32)
        part = m[0:n_pages]
        for group in (4, 4):
            part = jnp.sum(part.reshape(group, part.shape[0] // group, t, PAGE), axis=0)
        per_lane = jnp.sum(part, axis=0) + m[n_pages]
        return jnp.sum(per_lane, axis=1, keepdims=True)

    thr = _bisect_pairs(lambda x: count(keys_s[...] >= x[None]) >= topk, 32, INT_MIN, (t, 1))
    kall = keys_s[...]
    need = topk - count(kall > thr[None])
    thr1 = jnp.maximum(thr, INT_MIN + 1)[None]

    def tie_bound():
        return _bisect(lambda x: count((keys_s[...] == thr[None]) & (pos < x[None])) < need, 31, 0, (t, 1))

    has_tie = jnp.max(count(kall == thr1) - need) > 0
    xb = lax.cond(has_tie, tie_bound, lambda: jnp.full((t, 1), INT_MAX, I32))
    tied = jnp.where(pos <= xb[None], 1.0, 0.0)
    keys_s[...] = pltpu.bitcast(
        jnp.where(kall > thr1, 1.0, jnp.where(kall == thr1, tied, 0.0)).astype(F32), I32)

    sub_i = lax.broadcasted_iota(I32, (PAGE, PAGE), 0)
    lane_i = lax.broadcasted_iota(I32, (PAGE, PAGE), 1)
    upper = jnp.where(sub_i <= lane_i, 1.0, 0.0).astype(BF16)
    lower = jnp.where(lane_i <= sub_i, 1.0, 0.0).astype(BF16)
    ones = jnp.ones((PAGE, PAGE), BF16)
    slot = lax.broadcasted_iota(I32, (topk, PAGE), 0).astype(F32)
    tile_of_lane = lax.broadcasted_iota(I32, (topk, PAGE), 1).astype(F32)
    queries = range(t)
    mm = functools.partial(jnp.dot, preferred_element_type=F32)
    sel = [pltpu.bitcast(keys_s[0:n_pages, qq, :], F32).astype(BF16) for qq in queries]
    within = [mm(s, upper) for s in sel]
    cnt = [jnp.broadcast_to(w_[:, PAGE - 1:PAGE], (n_pages, PAGE)).astype(BF16) for w_ in within]
    running = [mm(lower, c) for c in cnt]
    before = [jnp.where(jnp.concatenate([r.T] * (topk // PAGE), axis=0) <= slot, 1.0, 0.0).astype(BF16)
              for r in running]
    tile_r = [mm(bf, ones) for bf in before]
    base_r = [mm(bf, c) for bf, c in zip(before, cnt)]
    onehot = [jnp.where(tr == tile_of_lane, 1.0, 0.0).astype(BF16) for tr in tile_r]
    within_r = [mm(oh, w_.astype(BF16)) for oh, w_ in zip(onehot, within)]
    lane_r = [mm(jnp.where(wr <= slot - br, 1.0, 0.0).astype(BF16), ones) for wr, br in zip(within_r, base_r)]
    valid = [slot < r[n_pages - 1:n_pages, :] for r in running]
    for qq in queries:
        pos_r = jnp.where(valid[qq], tile_r[qq] * PAGE + lane_r[qq], 0.0)
        idx_v[qq:qq + 1, :] = pos_r.T[0:1, :].astype(I32)
    valid_g = jnp.concatenate([jnp.where(v, 1.0, 0.0).T[0:1, :] for v in valid], axis=0)

    pos_g = idx_v[...]
    pt_row = jnp.broadcast_to(ptv_ref[0], (t, PAGE))
    tile_g = lax.shift_right_logical(pos_g, 7)
    page_g = jnp.concatenate([jnp.take_along_axis(pt_row, tile_g[:, s * PAGE:(s + 1) * PAGE], axis=1)
                              for s in range(sel_pages)], axis=1)
    rows_v[...] = (page_g + page_base) * PAGE_ROWS + (pos_g & (PAGE - 1)) * N_HEADS
    to_smem = pltpu.make_async_copy(rows_v, idx_sm, ssem)
    to_smem.start()
    to_smem.wait()

    def issue(j, _):
        for qq in range(t):
            src = pl.ds(pl.multiple_of(idx_sm[qq, j], N_HEADS), N_HEADS)
            dst = pl.ds(pl.multiple_of((qq * topk + j) * N_HEADS, N_HEADS), N_HEADS)
            pltpu.make_async_copy(ck_hbm.at[src], kg.at[dst], ksem).start()
            pltpu.make_async_copy(cv_hbm.at[src], vg.at[dst], vsem).start(priority=1)
        return 0

    lax.fori_loop(0, topk, issue, 0)

    n_new = LANES
    n_g = topk * N_HEADS

    def expand(x):
        per_group = LANES // N_HEADS
        lane_slot = lax.broadcasted_iota(I32, (t, LANES), 1) >> 3
        out = []
        for c in range((n_g + n_new) // LANES):
            src = x[:, (c * per_group // LANES) * LANES:(c * per_group // LANES + 1) * LANES]
            out.append(jnp.take_along_axis(src, (c * per_group) % LANES + lane_slot, axis=1))
        return jnp.concatenate(out, axis=1)

    qpos = past_len + lax.broadcasted_iota(I32, (t, topk + PAGE), 0)
    new_lane = lax.broadcasted_iota(I32, (t, PAGE), 1)
    key_pos = jnp.concatenate([pos_g, past_len + new_lane], axis=1)
    sel_new = pltpu.bitcast(keys_s[n_tiles - 1], F32)
    take = jnp.concatenate([valid_g, sel_new], axis=1)
    bucket_x = expand(_t5_bucket(qpos - key_pos).astype(F32))[:, 0:n_g + n_new]
    take_x = expand(take)[:, 0:n_g + n_new]
    head = lax.broadcasted_iota(I32, (t, n_g + n_new), 1) & (N_HEADS - 1)
    in_range = lax.broadcasted_iota(I32, (t, n_g + n_new), 1) < n_g + t * N_HEADS
    bias = jnp.zeros((t, n_g + n_new), F32)
    for h in range(N_HEADS):
        bias = jnp.where(head == h, _bias_of_bucket(bucket_x, relb_ref, h), bias)
    amask = jnp.where((take_x > 0.5) & in_range, 0.0, NEG_BIG)

    qf = q_ref[0].astype(F32)
    pad = jnp.zeros((LANES - t * N_HEADS, HEAD_DIM), F32)
    k_new = jnp.concatenate([kn_ref[0], pad], axis=0).astype(BF16)
    v_new = jnp.concatenate([vn_ref[0], pad], axis=0).astype(BF16)

    def own_head(n):
        return (lax.broadcasted_iota(I32, (N_HEADS, n), 0)
                == (lax.broadcasted_iota(I32, (N_HEADS, n), 1) & (N_HEADS - 1)))

    def gathered(buf, qq):
        return [buf[(qq * sel_pages + s) * PAGE_ROWS:(qq * sel_pages + s + 1) * PAGE_ROWS, :].astype(BF16)
                for s in range(sel_pages)]

    n_rows = t * topk * N_HEADS
    pltpu.make_async_copy(ck_hbm.at[pl.ds(0, n_rows)], kg, ksem).wait()
    lg_rows = []
    for qq in range(t):
        q_heads = jnp.concatenate([qf[qq:qq + 1, h * HEAD_DIM:(h + 1) * HEAD_DIM] for h in range(N_HEADS)],
                                  axis=0).astype(BF16)
        parts = []
        for x_bf in gathered(kg, qq) + [k_new]:
            r = lax.dot_general(q_heads, x_bf, nt_dims, preferred_element_type=F32)
            parts.append(jnp.sum(jnp.where(own_head(r.shape[1]), r, 0.0), axis=0, keepdims=True))
        lg_rows.append(jnp.concatenate(parts, axis=1))
    lg = jnp.concatenate(lg_rows, axis=0) * scale + bias + amask

    def per_head_allreduce(x, op):
        for shift in (8, 16, 32, 64):
            x = op(x, pltpu.roll(x, shift, axis=1))
        return x

    groups = (n_g + n_new) // LANES
    m_run = lg[:, 0:LANES]
    for s in range(1, groups):
        m_run = jnp.maximum(m_run, lg[:, s * LANES:(s + 1) * LANES])
    m_fin = per_head_allreduce(m_run, jnp.maximum)
    pr = jnp.exp(lg - jnp.concatenate([m_fin] * groups, axis=1))
    l_run = pr[:, 0:LANES]
    for s in range(1, groups):
        l_run = l_run + pr[:, s * LANES:(s + 1) * LANES]
    l_fin = per_head_allreduce(l_run, jnp.add)

    pltpu.make_async_copy(cv_hbm.at[pl.ds(0, n_rows)], vg, vsem).wait()
    for qq in range(t):
        acc = jnp.zeros((N_HEADS, HEAD_DIM), F32)
        lane0 = 0
        for x_bf in gathered(vg, qq) + [v_new]:
            n = x_bf.shape[0]
            p_heads = jnp.where(own_head(n), jnp.broadcast_to(pr[qq:qq + 1, lane0:lane0 + n], (N_HEADS, n)), 0.0)
            acc = acc + jnp.dot(p_heads.astype(BF16), x_bf, preferred_element_type=F32)
            lane0 += n
        acc_s[qq * N_HEADS:(qq + 1) * N_HEADS, :] = acc
    for h in range(N_HEADS):
        o_ref[0, :, h * HEAD_DIM:(h + 1) * HEAD_DIM] = (
            acc_s[pl.ds(h, t, stride=N_HEADS), :] / l_fin[:, h:h + 1]).astype(BF16)


def _sample_gather_attn(page_table, rel_bias, qb, qib, kw, k_new, v_new, cki, ck, cv, topk, page_base):
    b, t, _ = qb.shape
    n_pages = page_table.shape[1]
    past_len = n_pages * PAGE
    nqi = N_IDX_HEADS * IDX_DIM
    assert t == SUBLANES and n_pages == PAGE and topk % PAGE == 0 and n_pages % IDX_PAGES == 0
    per_b = lambda bi, pt: (bi, 0, 0)
    grid_spec = pltpu.PrefetchScalarGridSpec(
        num_scalar_prefetch=1,
        grid=(b,),
        in_specs=[pl.BlockSpec(memory_space=pltpu.SMEM),
                  pl.BlockSpec((1, 1, n_pages), per_b),
                  pl.BlockSpec((1, t, D_MODEL), per_b),
                  pl.BlockSpec((1, t, nqi), per_b),
                  pl.BlockSpec((1, t, LANES), per_b),
                  pl.BlockSpec((1, t * N_HEADS, HEAD_DIM), per_b),
                  pl.BlockSpec((1, t * N_HEADS, HEAD_DIM), per_b),
                  pl.BlockSpec(memory_space=pl.ANY),
                  pl.BlockSpec(memory_space=pl.ANY),
                  pl.BlockSpec(memory_space=pl.ANY)],
        out_specs=pl.BlockSpec((1, t, D_MODEL), per_b),
        scratch_shapes=[pltpu.VMEM((n_pages, IDX_DIM, PAGE), F32),
                        pltpu.VMEM((t * topk * N_HEADS, HEAD_DIM), F32),
                        pltpu.VMEM((t * topk * N_HEADS, HEAD_DIM), F32),
                        pltpu.VMEM((n_pages + 1, t, PAGE), I32),
                        pltpu.VMEM((t, topk), I32),
                        pltpu.VMEM((t, topk), I32),
                        pltpu.SMEM((t, topk), I32),
                        pltpu.VMEM((t * N_HEADS, HEAD_DIM), F32),
                        pltpu.SemaphoreType.DMA((n_pages // IDX_PAGES,)),
                        pltpu.SemaphoreType.DMA(()),
                        pltpu.SemaphoreType.DMA(()),
                        pltpu.SemaphoreType.DMA(())])
    return pl.pallas_call(
        functools.partial(_sample_gather_kernel, t=t, past_len=past_len, topk=topk, page_base=page_base),
        grid_spec=grid_spec,
        out_shape=jax.ShapeDtypeStruct((b, t, D_MODEL), BF16),
        compiler_params=_params(("arbitrary",)),
        name="sample_attn",
    )(page_table, rel_bias, page_table.reshape(b, 1, n_pages), qb, qib, kw, k_new, v_new, cki, ck, cv)


def kernel(x_prompt, x_sample, state_conv_a, state_conv_b, cache_k, cache_v, cache_kidx, page_table,
           norm_conv, w_in_conv, conv_a_w, conv_b_w, conv_b_bias, ln_b_gain, ln_b_bias, w_out_conv,
           norm_attn, w_in_attn, w_out_attn, rel_bias, norm_mlp, w_up, w_down, norm_final):
    depth = norm_mlp.shape[0]
    n_pool = cache_k.shape[1]
    past_len = page_table.shape[1] * PAGE
    nqkv = 3 * D_MODEL
    nqi = N_IDX_HEADS * IDX_DIM
    row = lambda a: a.reshape(1, -1)

    cki_flat = jnp.swapaxes(cache_kidx, 2, 3).reshape(-1, IDX_DIM, PAGE)
    ck_flat = cache_k.reshape(-1, HEAD_DIM)
    cv_flat = cache_v.reshape(-1, HEAD_DIM)
    dbias = _prompt_bias(rel_bias)

    def run(x, sample):
        b, t, _ = x.shape
        m = b * t
        tm = min(512, m)
        x2d = x.reshape(m, D_MODEL)
        new_a, new_b, new_k, new_v, new_ki = [], [], [], [], []
        for layer in range(depth):
            i = layer // 2
            last = layer == depth - 1
            if layer % 2 == 0:
                if sample:
                    buf_a, buf_b = state_conv_a[i], state_conv_b[i]
                else:
                    buf_a = jnp.zeros((b, CONV_A_WIDTH - 1, D_A), F32)
                    buf_b = jnp.zeros((b, CONV_B_WIDTH - 1, D_B), F32)
                sa = jnp.pad(buf_a, ((0, 0), (A_HALO - (CONV_A_WIDTH - 1), 0), (0, 0)))
                sb = jnp.pad(buf_b, ((0, 0), (B_HALO - (CONV_B_WIDTH - 1), 0), (0, 0)))
                gb, cx, u = _conv_in(x2d, row(norm_conv[i]), w_in_conv[i].astype(BF16), tm)
                tt = min(512, t)
                y, na, nb = _conv(gb.reshape(b, t, D_A), cx.reshape(b, t, D_A), u.reshape(b, t, D_B), sa, sb,
                                  conv_a_w[i], conv_b_w[i], row(conv_b_bias[i]), row(ln_b_gain[i]),
                                  row(ln_b_bias[i]), tt, min(64, tt))
                new_a.append(na)
                new_b.append(nb)
                mix = y.reshape(m, D_MODEL)
                w_out = w_out_conv[i]
            else:
                w_in = w_in_attn[i]
                wkw = jnp.pad(w_in[:, nqkv + nqi:], ((0, 0), (0, LANES - IDX_DIM - N_IDX_HEADS)))
                wvt = w_in[:, 2 * D_MODEL:nqkv].T.astype(BF16)
                outs = _attn_in(x2d, row(norm_attn[i]), w_in[:, :nqkv].astype(BF16), wvt,
                                w_in[:, nqkv:nqkv + nqi].astype(BF16), wkw.astype(BF16), tm, t, not sample)
                qb, k, v, qib, kw = outs[:5]
                shp = lambda a: a.reshape(b, t, a.shape[-1])
                if sample:
                    topk = min(TOPK_MAX, (past_len + t) // 4)
                    o = _sample_gather_attn(page_table, rel_bias, shp(qb), shp(qib), shp(kw),
                                            k.reshape(b, t * N_HEADS, HEAD_DIM),
                                            v.reshape(b, t * N_HEADS, HEAD_DIM),
                                            cki_flat, ck_flat, cv_flat, topk, i * n_pool)
                else:
                    topk = min(TOPK_MAX, t // 4)
                    kb, vt = outs[5:]
                    o = _prompt_attn(rel_bias, shp(qb), shp(qib), shp(kw), shp(kb), vt, dbias, topk)
                new_k.append(k.reshape(b, t, N_HEADS, HEAD_DIM))
                new_v.append(v.reshape(b, t, N_HEADS, HEAD_DIM))
                new_ki.append(kw[:, :IDX_DIM].reshape(b, t, IDX_DIM))
                mix = o.reshape(m, D_MODEL)
                w_out = w_out_attn[i]
            x2d = _out_mlp(mix, x2d, w_out.astype(BF16), row(norm_mlp[layer]), w_up[layer].astype(BF16),
                           w_down[layer].astype(BF16), row(norm_final), tm, last)
        return (x2d.reshape(b, t, D_MODEL), jnp.stack(new_a), jnp.stack(new_b),
                jnp.stack(new_k), jnp.stack(new_v), jnp.stack(new_ki))

    y_p, ca_p, cb_p, k_p, v_p, ki_p = run(x_prompt, False)
    y_s, ca_s, cb_s, k_s, v_s, ki_s = run(x_sample, True)
    return (y_p, y_s, ca_p, cb_p, k_p, v_p, ki_p, ca_s, cb_s, k_s, v_s, ki_s)
```
